```python
import jax
import jax.numpy as jnp
from jax import lax
import numpy as np


D_MODEL = 1024
BATCH = 4
SEQ = 4096
DEPTH = 2

CHUNK = 64
Q_BLOCK = 128

N_BRANCH = 4
BRANCH_WIDTH = 512

SC_WIDTH = 512
SC_CONV_LEN = 3

CF_WIDTH = 512
CF_CONV_LEN = 31

MLA_HEADS = 8
MLA_NOPE = 64
MLA_ROPE = 32
MLA_V = 64
MLA_Q_RANK = 256
MLA_KV_RANK = 128
ROPE_THETA = 10000.0

SB_HEADS = 8
SB_HEAD_DIM = 64

D_FF = 3584
N_EXPERTS = 8
TOP_K = 2
N_DENSE = (DEPTH + 1) // 2
N_MOE = DEPTH // 2

DN_ALPHA = (2 * DEPTH) ** 0.25
DN_BETA = (8 * DEPTH) ** -0.25

LN_EPS = 1e-5
RMS_EPS = 1e-6

IN_SIZES = (
    3 * SC_WIDTH,
    2 * CF_WIDTH,
    MLA_Q_RANK,
    MLA_KV_RANK,
    MLA_ROPE,
    3 * SB_HEADS * SB_HEAD_DIM,
    N_BRANCH * D_MODEL,
)
IN_COLS = sum(IN_SIZES)

kernel_name = "hybrid_gated_conv_mla_stickbreak_moe"


def _layer_norm(x, g, b):
    xf = x.astype(jnp.float32)
    mu = jnp.mean(xf, axis=-1, keepdims=True)
    var = jnp.mean(jnp.square(xf - mu), axis=-1, keepdims=True)
    y = (xf - mu) * lax.rsqrt(var + LN_EPS)
    return (y * g.astype(jnp.float32) + b.astype(jnp.float32)).astype(x.dtype)


def _rms_norm(x, g):
    xf = x.astype(jnp.float32)
    y = xf * lax.rsqrt(jnp.mean(jnp.square(xf), axis=-1, keepdims=True) + RMS_EPS)
    return (y * g.astype(jnp.float32)).astype(x.dtype)


def _rope_tables(positions):
    half = MLA_ROPE // 2
    inv_freq = 1.0 / (ROPE_THETA ** (jnp.arange(half, dtype=jnp.float32) * (2.0 / MLA_ROPE)))
    ang = positions.astype(jnp.float32)[..., None] * inv_freq
    return jnp.cos(ang), jnp.sin(ang)


def _apply_rope(x, cos, sin):
    half = x.shape[-1] // 2
    x1, x2 = x[..., :half], x[..., half:]
    cos = cos.astype(x.dtype)
    sin = sin.astype(x.dtype)
    return jnp.concatenate([x1 * cos - x2 * sin, x1 * sin + x2 * cos], axis=-1)


def _causal_dwconv(u, w):
    n_taps = w.shape[0]
    u_pad = jnp.pad(u, ((0, 0), (n_taps - 1, 0), (0, 0)))
    return lax.conv_general_dilated(
        u_pad, w[:, None, :].astype(u.dtype), window_strides=(1,), padding='VALID',
        dimension_numbers=('NWC', 'WIO', 'NWC'), feature_group_count=u.shape[-1])


def _split_cols(proj):
    parts = []
    start = 0
    for size in IN_SIZES:
        parts.append(proj[..., start:start + size])
        start += size
    return parts


def _short_gated_conv(b_gate, c_gate, h, w_conv):
    return b_gate * _causal_dwconv(c_gate * h, w_conv)


def _conformer_conv(val, gate, w_dw, b_dw, ln_g, ln_b):
    u = val * jax.nn.sigmoid(gate)
    u = _causal_dwconv(u, w_dw) + b_dw.astype(u.dtype)
    return jax.nn.silu(_layer_norm(u, ln_g, ln_b))


def _mla_attention(c_q, c_kv, k_rope_raw, cos, sin, q_norm, w_uq, kv_norm, w_ukv):
    bsz, seq, _ = c_q.shape
    q = (_rms_norm(c_q, q_norm) @ w_uq).reshape(bsz, seq, MLA_HEADS, MLA_NOPE + MLA_ROPE)
    q_nope = q[..., :MLA_NOPE]
    q_rope = _apply_rope(q[..., MLA_NOPE:], cos[:, :, None, :], sin[:, :, None, :])
    kv = (_rms_norm(c_kv, kv_norm) @ w_ukv).reshape(bsz, seq, MLA_HEADS, MLA_NOPE + MLA_V)
    k_nope, v = kv[..., :MLA_NOPE], kv[..., MLA_NOPE:]
    k_rope = _apply_rope(k_rope_raw, cos, sin)
    scale = (MLA_NOPE + MLA_ROPE) ** -0.5
    outs = []
    for blk in range(seq // Q_BLOCK):
        q0 = blk * Q_BLOCK
        q1 = q0 + Q_BLOCK
        s = (jnp.einsum('bqhd,bkhd->bhqk', q_nope[:, q0:q1], k_nope[:, :q1])
             + jnp.einsum('bqhd,bkd->bhqk', q_rope[:, q0:q1], k_rope[:, :q1])).astype(jnp.float32) * scale
        tq = jnp.arange(q0, q1)
        tk = jnp.arange(q1)
        allowed = (tk[None, :] // CHUNK) <= (tq[:, None] // CHUNK)
        s = jnp.where(allowed, s, -jnp.inf)
        p = jax.nn.softmax(s, axis=-1).astype(v.dtype)
        outs.append(jnp.einsum('bhqk,bkhd->bqhd', p, v[:, :q1]))
    return jnp.concatenate(outs, axis=1).reshape(bsz, seq, MLA_HEADS * MLA_V)


def _stick_breaking_attention(qkv):
    bsz, seq, _ = qkv.shape
    r = qkv.reshape(bsz, seq, 3, SB_HEADS, SB_HEAD_DIM)
    q, k, v = r[:, :, 0], r[:, :, 1], r[:, :, 2]
    scale = SB_HEAD_DIM ** -0.5
    outs = []
    for blk in range(seq // Q_BLOCK):
        q0 = blk * Q_BLOCK
        q1 = q0 + Q_BLOCK
        z = jnp.einsum('bqhd,bkhd->bhqk', q[:, q0:q1], k[:, :q1]).astype(jnp.float32) * scale
        tq = jnp.arange(q0, q1)
        tk = jnp.arange(q1)
        before = tk[None, :] < tq[:, None]
        log_stay = jnp.where(before, jax.nn.log_sigmoid(-z), 0.0)
        log_pass = lax.cumsum(log_stay, axis=3, reverse=True) - log_stay
        a = jnp.where(before, jnp.exp(jax.nn.log_sigmoid(z) + log_pass), 0.0).astype(v.dtype)
        outs.append(jnp.einsum('bhqk,bkhd->bqhd', a, v[:, :q1]))
    return jnp.concatenate(outs, axis=1).reshape(bsz, seq, SB_HEADS * SB_HEAD_DIM)


def _mixer_sublayer(x, cos, sin, w_in, b_gate, sc_conv, cf_conv, cf_conv_bias, cf_ln_g, cf_ln_b,
                    mla_q_norm, mla_w_uq, mla_kv_norm, mla_w_ukv, w_branch, w_out):
    bsz, seq, d = x.shape
    proj = x @ w_in
    sc_in, cf_in, c_q, c_kv, k_rope_raw, sb_in, gate_pre = _split_cols(proj)
    sc_b, sc_c, sc_h = jnp.split(sc_in, 3, axis=-1)
    cf_val, cf_gate = jnp.split(cf_in, 2, axis=-1)
    branches = (
        _short_gated_conv(sc_b, sc_c, sc_h, sc_conv),
        _conformer_conv(cf_val, cf_gate, cf_conv, cf_conv_bias, cf_ln_g, cf_ln_b),
        _mla_attention(c_q, c_kv, k_rope_raw, cos, sin, mla_q_norm, mla_w_uq, mla_kv_norm, mla_w_ukv),
        _stick_breaking_attention(sb_in),
    )
    gates = jax.nn.sigmoid(gate_pre + b_gate).reshape(bsz, seq, N_BRANCH, d)
    merged = gates[:, :, 0] * (branches[0] @ w_branch[0])
    for n in range(1, N_BRANCH):
        merged = merged + gates[:, :, n] * (branches[n] @ w_branch[n])
    return merged @ w_out


def _swiglu(x, w_gate, w_up, w_down):
    return (jax.nn.silu(x @ w_gate) * (x @ w_up)) @ w_down


def _moe_swiglu(x, w_router, w_gate, w_up, w_down):
    bsz, seq, d = x.shape
    xf = x.reshape(bsz * seq, d)
    logits = (xf @ w_router).astype(jnp.float32)
    top_vals, top_idx = lax.top_k(logits, TOP_K)
    top_w = jax.nn.softmax(top_vals, axis=-1)
    combine = jnp.sum(jax.nn.one_hot(top_idx, N_EXPERTS, dtype=jnp.float32) * top_w[..., None], axis=1)
    out = jnp.zeros_like(xf)
    for e in range(N_EXPERTS):
        out = out + combine[:, e:e + 1].astype(xf.dtype) * _swiglu(xf, w_gate[e], w_up[e], w_down[e])
    return out.reshape(bsz, seq, d)


def setup_inputs(seed: int = 0) -> dict:
    key = jax.random.key(seed)
    ks = jax.random.split(key, 32)
    L = DEPTH
    D = D_MODEL

    def nrm(i, shape, scale):
        return jax.random.normal(ks[i], shape, jnp.float32) * scale

    x = nrm(0, (BATCH, SEQ, D), 1.0)
    offsets = jax.random.randint(ks[1], (BATCH, 1), 0, 4 * SEQ)
    positions = (offsets + jnp.arange(SEQ)[None, :]).astype(jnp.int32)
    return {
        'x': x,
        'positions': positions,
        'w_in': nrm(2, (L, D, IN_COLS), D ** -0.5),
        'b_gate': nrm(3, (L, N_BRANCH * D), 0.02),
        'sc_conv': nrm(4, (L, SC_CONV_LEN, SC_WIDTH), SC_CONV_LEN ** -0.5),
        'cf_conv': nrm(5, (L, CF_CONV_LEN, CF_WIDTH), CF_CONV_LEN ** -0.5),
        'cf_conv_bias': nrm(6, (L, CF_WIDTH), 0.02),
        'cf_ln_g': 1.0 + nrm(7, (L, CF_WIDTH), 0.02),
        'cf_ln_b': nrm(8, (L, CF_WIDTH), 0.02),
        'mla_q_norm': 1.0 + nrm(9, (L, MLA_Q_RANK), 0.02),
        'mla_w_uq': nrm(10, (L, MLA_Q_RANK, MLA_HEADS * (MLA_NOPE + MLA_ROPE)), MLA_Q_RANK ** -0.5),
        'mla_kv_norm': 1.0 + nrm(11, (L, MLA_KV_RANK), 0.02),
        'mla_w_ukv': nrm(12, (L, MLA_KV_RANK, MLA_HEADS * (MLA_NOPE + MLA_V)), MLA_KV_RANK ** -0.5),
        'w_branch': nrm(13, (L, N_BRANCH, BRANCH_WIDTH, D), BRANCH_WIDTH ** -0.5 * DN_BETA),
        'w_out': nrm(14, (L, D, D), D ** -0.5 * DN_BETA),
        'ln_mix_g': 1.0 + nrm(15, (L, D), 0.02),
        'ln_mix_b': nrm(16, (L, D), 0.02),
        'ln_ffn_g': 1.0 + nrm(17, (L, D), 0.02),
        'ln_ffn_b': nrm(18, (L, D), 0.02),
        'ffn_w_gate': nrm(19, (N_DENSE, D, D_FF), D ** -0.5),
        'ffn_w_up': nrm(20, (N_DENSE, D, D_FF), D ** -0.5),
        'ffn_w_down': nrm(21, (N_DENSE, D_FF, D), D_FF ** -0.5 * DN_BETA),
        'router_w': nrm(22, (N_MOE, D, N_EXPERTS), D ** -0.5),
        'exp_w_gate': nrm(23, (N_MOE, N_EXPERTS, D, D_FF), D ** -0.5),
        'exp_w_up': nrm(24, (N_MOE, N_EXPERTS, D, D_FF), D ** -0.5),
        'exp_w_down': nrm(25, (N_MOE, N_EXPERTS, D_FF, D), D_FF ** -0.5 * DN_BETA),
    }


def reference(x, positions, w_in, b_gate, sc_conv, cf_conv, cf_conv_bias, cf_ln_g, cf_ln_b,
              mla_q_norm, mla_w_uq, mla_kv_norm, mla_w_ukv, w_branch, w_out,
              ln_mix_g, ln_mix_b, ln_ffn_g, ln_ffn_b, ffn_w_gate, ffn_w_up, ffn_w_down,
              router_w, exp_w_gate, exp_w_up, exp_w_down):
    cos, sin = _rope_tables(positions)
    for layer in range(DEPTH):
        y = _mixer_sublayer(x, cos, sin, w_in[layer], b_gate[layer], sc_conv[layer], cf_conv[layer],
                            cf_conv_bias[layer], cf_ln_g[layer], cf_ln_b[layer], mla_q_norm[layer],
                            mla_w_uq[layer], mla_kv_norm[layer], mla_w_ukv[layer], w_branch[layer],
                            w_out[layer])
        x = _layer_norm(DN_ALPHA * x + y, ln_mix_g[layer], ln_mix_b[layer])
        if layer % 2 == 0:
            i = layer // 2
            f = _swiglu(x, ffn_w_gate[i], ffn_w_up[i], ffn_w_down[i])
        else:
            i = layer // 2
            f = _moe_swiglu(x, router_w[i], exp_w_gate[i], exp_w_up[i], exp_w_down[i])
        x = _layer_norm(DN_ALPHA * x + f, ln_ffn_g[layer], ln_ffn_b[layer])
    return x
```

```python
import functools

import jax
import jax.numpy as jnp
from jax import lax
from jax.experimental import pallas as pl
from jax.experimental.pallas import tpu as pltpu

F32 = jnp.float32
BF16 = jnp.bfloat16

D_MODEL = 1024
DEPTH = 2
CHUNK = 64
BRANCH_WIDTH = 512
SC_CONV_LEN = 3
CF_CONV_LEN = 31
MLA_HEADS = 8
MLA_NOPE = 64
MLA_ROPE = 32
MLA_V = 64
MLA_Q_RANK = 256
MLA_KV_RANK = 128
ROPE_THETA = 10000.0
SB_HEADS = 8
SB_HEAD_DIM = 64
D_FF = 3584
N_EXPERTS = 8
DN_ALPHA = (2 * DEPTH) ** 0.25
LN_EPS = 1e-5
RMS_EPS = 1e-6

LANES = 128
HEAD_PAD = 128
CONV_HALO = 32
VMEM_LIMIT = 56 * 1024 * 1024

OFF_SCB, OFF_SCC, OFF_SCH = 0, 512, 1024
OFF_CFV, OFF_CFG = 1536, 2048
OFF_CQ = 2560
OFF_CKV = 2816
OFF_KR = 2944
OFF_KRR = 3072
OFF_SBQ, OFF_SBK, OFF_SBV = 3200, 3712, 4224
N_MIX = 4736

TM_PROJ = 512
TS_CONV = 512
TS_PREP = 512
TQ_ATT = 256
TM_MERGE = 512
TM_FFN = 512
TF_FFN = 1792
TM_ROUTE = 512
G_GATHER = 512
TC_COMBINE = 256

_NT = (((1,), (1,)), ((), ()))


def _params(*sem):
    return pltpu.CompilerParams(dimension_semantics=sem, vmem_limit_bytes=VMEM_LIMIT)


def _sigmoid(x):
    return 1.0 / (1.0 + jnp.exp(-x))


def _layer_norm_rows(v, g, b):
    mu = jnp.mean(v, axis=-1, keepdims=True)
    c = v - mu
    var = jnp.mean(c * c, axis=-1, keepdims=True)
    return c * lax.rsqrt(var + LN_EPS) * g + b


def _rope_kernel(pos_ref, freq_ref, cos_ref, sin_ref):
    ang = pos_ref[0] * freq_ref[...]
    cos_ref[0] = jnp.cos(ang)
    sin_ref[0] = jnp.sin(ang)


def _rope_tables(positions):
    bsz, seq = positions.shape
    half = MLA_ROPE // 2
    inv_freq = 1.0 / (ROPE_THETA ** (jnp.arange(half, dtype=F32) * (2.0 / MLA_ROPE)))
    freq = jnp.concatenate([jnp.zeros((MLA_NOPE,), F32), inv_freq, inv_freq,
                            jnp.zeros((HEAD_PAD - MLA_NOPE - MLA_ROPE,), F32)])[None, :]
    pos = positions.astype(F32)[..., None]
    ts = TS_PREP
    return pl.pallas_call(
        _rope_kernel,
        grid=(bsz, seq // ts),
        in_specs=[pl.BlockSpec((1, ts, 1), lambda b, i: (b, i, 0)),
                  pl.BlockSpec((1, HEAD_PAD), lambda b, i: (0, 0))],
        out_specs=[pl.BlockSpec((1, ts, HEAD_PAD), lambda b, i: (b, i, 0))] * 2,
        out_shape=[jax.ShapeDtypeStruct((bsz, seq, HEAD_PAD), F32)] * 2,
        compiler_params=_params("parallel", "parallel"),
        name="rope_tables",
    )(pos, freq)


def _inproj_kernel(x_ref, w_ref, o_ref):
    xb = x_ref[...].astype(BF16)
    for c0 in range(0, N_MIX, 512):
        c1 = min(c0 + 512, N_MIX)
        o_ref[:, c0:c1] = jnp.dot(xb, w_ref[:, c0:c1], preferred_element_type=F32).astype(BF16)


def _inproj(x2, w_mix):
    n = x2.shape[0]
    tm = TM_PROJ
    return pl.pallas_call(
        _inproj_kernel,
        grid=(n // tm,),
        in_specs=[pl.BlockSpec((tm, D_MODEL), lambda i: (i, 0)),
                  pl.BlockSpec((D_MODEL, N_MIX), lambda i: (0, 0))],
        out_specs=pl.BlockSpec((tm, N_MIX), lambda i: (i, 0)),
        out_shape=jax.ShapeDtypeStruct((n, N_MIX), BF16),
        compiler_params=_params("parallel"),
        name="in_proj",
    )(x2, w_mix)


def _conv_kernel(b_ref, c_ref, h_ref, v_ref, g_ref, ch_ref, hh_ref, vh_ref, gh_ref,
                 wsc_ref, wcf_ref, bcf_ref, lng_ref, lnb_ref, o0_ref, o1_ref, buf, *, ts):
    has_history = pl.program_id(1) > 0
    halo = CONV_HALO

    hist = ch_ref[0].astype(F32) * hh_ref[0].astype(F32)
    buf[0:halo, :] = jnp.where(has_history, hist, 0.0)
    buf[halo:halo + ts, :] = c_ref[0].astype(F32) * h_ref[0].astype(F32)
    acc = None
    for l in range(SC_CONV_LEN):
        off = halo - (SC_CONV_LEN - 1) + l
        term = wsc_ref[l:l + 1, :] * buf[off:off + ts, :]
        acc = term if acc is None else acc + term
    o0_ref[0] = (b_ref[0].astype(F32) * acc).astype(BF16)

    hist = vh_ref[0].astype(F32) * _sigmoid(gh_ref[0].astype(F32))
    buf[0:halo, :] = jnp.where(has_history, hist, 0.0)
    buf[halo:halo + ts, :] = v_ref[0].astype(F32) * _sigmoid(g_ref[0].astype(F32))
    acc = None
    for l in range(CF_CONV_LEN):
        off = halo - (CF_CONV_LEN - 1) + l
        term = wcf_ref[l:l + 1, :] * buf[off:off + ts, :]
        acc = term if acc is None else acc + term
    y = _layer_norm_rows(acc + bcf_ref[...], lng_ref[...], lnb_ref[...])
    o1_ref[0] = (y * _sigmoid(y)).astype(BF16)


def _conv_branches(proj3, sc_conv, cf_conv, cf_bias, cf_ln_g, cf_ln_b):
    bsz, seq, _ = proj3.shape
    ts = TS_CONV
    w = BRANCH_WIDTH
    hb = ts // CONV_HALO

    def main(col):
        return pl.BlockSpec((1, ts, w), lambda b, i: (b, i, col))

    def halo(col):
        return pl.BlockSpec((1, CONV_HALO, w), lambda b, i: (b, jnp.maximum(i * hb - 1, 0), col))

    def vec(rows):
        return pl.BlockSpec((rows, w), lambda b, i: (0, 0))

    return pl.pallas_call(
        functools.partial(_conv_kernel, ts=ts),
        grid=(bsz, seq // ts),
        in_specs=[main(OFF_SCB // w), main(OFF_SCC // w), main(OFF_SCH // w),
                  main(OFF_CFV // w), main(OFF_CFG // w),
                  halo(OFF_SCC // w), halo(OFF_SCH // w), halo(OFF_CFV // w), halo(OFF_CFG // w),
                  vec(SC_CONV_LEN), vec(CF_CONV_LEN), vec(1), vec(1), vec(1)],
        out_specs=[pl.BlockSpec((1, ts, w), lambda b, i: (b, i, 0))] * 2,
        out_shape=[jax.ShapeDtypeStruct((bsz, seq, w), BF16)] * 2,
        scratch_shapes=[pltpu.VMEM((CONV_HALO + ts, w), F32)],
        compiler_params=_params("parallel", "parallel"),
        name="conv_branches",
    )(proj3, proj3, proj3, proj3, proj3, proj3, proj3, proj3, proj3,
      sc_conv, cf_conv, cf_bias[None, :], cf_ln_g[None, :], cf_ln_b[None, :])


def _mla_prep_kernel(cq_ref, ckv_ref, kr_ref, krr_ref, cos_ref, sin_ref, qn_ref, kvn_ref,
                     wq_ref, wqr_ref, wk_ref, wv_ref, q_ref, k_ref, v_ref):
    cos = cos_ref[0]
    sin = sin_ref[0]
    cos8 = jnp.concatenate([cos] * MLA_HEADS, axis=1)
    sin8 = jnp.concatenate([sin] * MLA_HEADS, axis=1)
    scale = (MLA_NOPE + MLA_ROPE) ** -0.5

    cq = cq_ref[0].astype(F32)
    nq = cq * lax.rsqrt(jnp.mean(cq * cq, axis=-1, keepdims=True) + RMS_EPS) * qn_ref[...]
    nq = nq.astype(BF16)
    q1 = jnp.dot(nq, wq_ref[...], preferred_element_type=F32)
    q2 = jnp.dot(nq, wqr_ref[...], preferred_element_type=F32)
    q_ref[0] = ((q1 * cos8 + q2 * sin8) * scale).astype(BF16)

    ckv = ckv_ref[0].astype(F32)
    nkv = ckv * lax.rsqrt(jnp.mean(ckv * ckv, axis=-1, keepdims=True) + RMS_EPS) * kvn_ref[...]
    nkv = nkv.astype(BF16)
    k_rope = kr_ref[0].astype(F32) * cos + krr_ref[0].astype(F32) * sin
    k_nope = jnp.dot(nkv, wk_ref[...], preferred_element_type=F32)
    k_ref[0] = (k_nope + jnp.concatenate([k_rope] * MLA_HEADS, axis=1)).astype(BF16)
    v_ref[0] = jnp.dot(nkv, wv_ref[...], preferred_element_type=F32).astype(BF16)


def _mla_prep(proj3, cos, sin, q_norm, kv_norm, wq, wqr, wk, wv):
    bsz, seq, _ = proj3.shape
    ts = TS_PREP
    hw = MLA_HEADS * HEAD_PAD
    vw = MLA_HEADS * MLA_V

    def const(shape):
        return pl.BlockSpec(shape, lambda b, i: (0, 0))

    return pl.pallas_call(
        _mla_prep_kernel,
        grid=(bsz, seq // ts),
        in_specs=[pl.BlockSpec((1, ts, MLA_Q_RANK), lambda b, i: (b, i, OFF_CQ // MLA_Q_RANK)),
                  pl.BlockSpec((1, ts, MLA_KV_RANK), lambda b, i: (b, i, OFF_CKV // MLA_KV_RANK)),
                  pl.BlockSpec((1, ts, HEAD_PAD), lambda b, i: (b, i, OFF_KR // HEAD_PAD)),
                  pl.BlockSpec((1, ts, HEAD_PAD), lambda b, i: (b, i, OFF_KRR // HEAD_PAD)),
                  pl.BlockSpec((1, ts, HEAD_PAD), lambda b, i: (b, i, 0)),
                  pl.BlockSpec((1, ts, HEAD_PAD), lambda b, i: (b, i, 0)),
                  const((1, MLA_Q_RANK)), const((1, MLA_KV_RANK)),
                  const((MLA_Q_RANK, hw)), const((MLA_Q_RANK, hw)),
                  const((MLA_KV_RANK, hw)), const((MLA_KV_RANK, vw))],
        out_specs=[pl.BlockSpec((1, ts, hw), lambda b, i: (b, i, 0)),
                   pl.BlockSpec((1, ts, hw), lambda b, i: (b, i, 0)),
                   pl.BlockSpec((1, ts, vw), lambda b, i: (b, i, 0))],
        out_shape=[jax.ShapeDtypeStruct((bsz, seq, hw), BF16),
                   jax.ShapeDtypeStruct((bsz, seq, hw), BF16),
                   jax.ShapeDtypeStruct((bsz, seq, vw), BF16)],
        compiler_params=_params("parallel", "parallel"),
        name="mla_prep",
    )(proj3, proj3, proj3, proj3, cos, sin, q_norm[None, :], kv_norm[None, :], wq, wqr, wk, wv)


def _mla_attn_kernel(q_ref, k_ref, v_ref, o_ref, m_ref, l_ref, acc_ref, *, tq):
    qi = pl.program_id(2)
    row = lax.broadcasted_iota(jnp.int32, (tq, tq), 0)
    col = lax.broadcasted_iota(jnp.int32, (tq, tq), 1)
    allowed = (col // CHUNK) <= (row // CHUNK)
    outs = []
    for hh in range(2):
        lo = hh * HEAD_PAD
        qh = q_ref[0, :, lo:lo + HEAD_PAD]
        m_ref[...] = jnp.full(m_ref.shape, -jnp.inf, F32)
        l_ref[...] = jnp.zeros(l_ref.shape, F32)
        acc_ref[...] = jnp.zeros(acc_ref.shape, F32)

        def block(kstart, masked, lo=lo, qh=qh):
            kblk = k_ref[0, pl.ds(kstart, tq), lo:lo + HEAD_PAD]
            vblk = v_ref[0, pl.ds(kstart, tq), :]
            s = lax.dot_general(qh, kblk, _NT, preferred_element_type=F32)
            if masked:
                s = jnp.where(allowed, s, -jnp.inf)
            m_prev = m_ref[...]
            m_new = jnp.maximum(m_prev, jnp.max(s, axis=1, keepdims=True))
            alpha = jnp.exp(m_prev - m_new)
            p = jnp.exp(s - m_new)
            l_ref[...] = alpha * l_ref[...] + jnp.sum(p, axis=1, keepdims=True)
            acc_ref[...] = alpha * acc_ref[...] + jnp.dot(p.astype(BF16), vblk,
                                                           preferred_element_type=F32)
            m_ref[...] = m_new

        def body(i, carry):
            block(pl.multiple_of(i * tq, tq), False)
            return carry

        lax.fori_loop(0, qi, body, 0)
        block(pl.multiple_of(qi * tq, tq), True)
        outs.append(acc_ref[...] / l_ref[...])
    lane = lax.broadcasted_iota(jnp.int32, (tq, 2 * MLA_V), 1)
    o_ref[0] = jnp.where(lane < MLA_V, outs[0], outs[1]).astype(BF16)


def _mla_attention(q, k, v):
    bsz, seq, _ = q.shape
    tq = TQ_ATT
    pairs = MLA_HEADS // 2
    return pl.pallas_call(
        functools.partial(_mla_attn_kernel, tq=tq),
        grid=(bsz, pairs, seq // tq),
        in_specs=[pl.BlockSpec((1, tq, 2 * HEAD_PAD), lambda b, h, i: (b, i, h)),
                  pl.BlockSpec((1, seq, 2 * HEAD_PAD), lambda b, h, i: (b, 0, h)),
                  pl.BlockSpec((1, seq, 2 * MLA_V), lambda b, h, i: (b, 0, h))],
        out_specs=pl.BlockSpec((1, tq, 2 * MLA_V), lambda b, h, i: (b, i, h)),
        out_shape=jax.ShapeDtypeStruct((bsz, seq, MLA_HEADS * MLA_V), BF16),
        scratch_shapes=[pltpu.VMEM((tq, 1), F32), pltpu.VMEM((tq, 1), F32),
                        pltpu.VMEM((tq, 2 * MLA_V), F32)],
        compiler_params=_params("parallel", "parallel", "arbitrary"),
        name="mla_attention",
    )(q, k, v)


def _sb_attn_kernel(q_ref, k_ref, v_ref, u_ref, o_ref, acc_ref, carry_ref, *, tq):
    qi = pl.program_id(2)
    row = lax.broadcasted_iota(jnp.int32, (tq, tq), 0)
    col = lax.broadcasted_iota(jnp.int32, (tq, tq), 1)
    before = col < row
    lane = lax.broadcasted_iota(jnp.int32, (tq, 2 * SB_HEAD_DIM), 1)
    q = q_ref[0]
    reps = tq // LANES
    outs = []
    for hh in range(2):
        in_head = (lane >= hh * SB_HEAD_DIM) & (lane < (hh + 1) * SB_HEAD_DIM)
        qm = jnp.where(in_head, q, jnp.zeros_like(q))
        acc_ref[...] = jnp.zeros(acc_ref.shape, F32)
        carry_ref[...] = jnp.zeros(carry_ref.shape, F32)

        def block(kstart, masked, qm=qm):
            kblk = k_ref[0, pl.ds(kstart, tq), :]
            vblk = v_ref[0, pl.ds(kstart, tq), :]
            nz = lax.dot_general(qm, kblk, _NT, preferred_element_type=F32)
            log_stay = jnp.minimum(nz, 0.0) - jnp.log(1.0 + jnp.exp(-jnp.abs(nz)))
            if masked:
                log_stay = jnp.where(before, log_stay, 0.0)
            sums = jnp.dot(log_stay.astype(BF16), u_ref[...], preferred_element_type=F32)
            carry = carry_ref[...]
            wide = jnp.concatenate([carry] * reps, axis=1)
            a = jnp.exp(sums[:, :tq] + (wide - nz))
            if masked:
                a = jnp.where(before, a, 0.0)
            acc_ref[...] += jnp.dot(a.astype(BF16), vblk, preferred_element_type=F32)
            carry_ref[...] = carry + sums[:, tq:]

        block(pl.multiple_of(qi * tq, tq), True)

        def body(i, c):
            block(pl.multiple_of((qi - 1 - i) * tq, tq), False)
            return c

        lax.fori_loop(0, qi, body, 0)
        outs.append(acc_ref[...])
    o_ref[0] = jnp.where(lane < SB_HEAD_DIM, outs[0], outs[1]).astype(BF16)


def _sb_attention(proj3):
    bsz, seq, _ = proj3.shape
    tq = TQ_ATT
    pairs = SB_HEADS // 2
    pw = 2 * SB_HEAD_DIM
    r = lax.broadcasted_iota(jnp.int32, (tq, tq + LANES), 0)
    c = lax.broadcasted_iota(jnp.int32, (tq, tq + LANES), 1)
    suffix = ((r >= c) | (c >= tq)).astype(BF16)
    return pl.pallas_call(
        functools.partial(_sb_attn_kernel, tq=tq),
        grid=(bsz, pairs, seq // tq),
        in_specs=[pl.BlockSpec((1, tq, pw), lambda b, h, i: (b, i, OFF_SBQ // pw + h)),
                  pl.BlockSpec((1, seq, pw), lambda b, h, i: (b, 0, OFF_SBK // pw + h)),
                  pl.BlockSpec((1, seq, pw), lambda b, h, i: (b, 0, OFF_SBV // pw + h)),
                  pl.BlockSpec((tq, tq + LANES), lambda b, h, i: (0, 0))],
        out_specs=pl.BlockSpec((1, tq, pw), lambda b, h, i: (b, i, h)),
        out_shape=jax.ShapeDtypeStruct((bsz, seq, SB_HEADS * SB_HEAD_DIM), BF16),
        scratch_shapes=[pltpu.VMEM((tq, pw), F32), pltpu.VMEM((tq, LANES), F32)],
        compiler_params=_params("parallel", "parallel", "arbitrary"),
        name="sb_attention",
    )(proj3, proj3, proj3, suffix)


def _merge_kernel(x_ref, b0_ref, b1_ref, b2_ref, b3_ref, wg_ref, bg_ref, wb_ref, wo_ref,
                  lg_ref, lb_ref, o_ref):
    x = x_ref[...]
    xb = x.astype(BF16)
    merged = None
    for n, br in enumerate((b0_ref, b1_ref, b2_ref, b3_ref)):
        c0 = n * D_MODEL
        pre = jnp.dot(xb, wg_ref[:, c0:c0 + D_MODEL], preferred_element_type=F32)
        gate = _sigmoid(pre + bg_ref[:, c0:c0 + D_MODEL])
        term = gate * jnp.dot(br[...], wb_ref[n], preferred_element_type=F32)
        merged = term if merged is None else merged + term
    y = jnp.dot(merged.astype(BF16), wo_ref[...], preferred_element_type=F32)
    o_ref[...] = _layer_norm_rows(DN_ALPHA * x + y, lg_ref[...], lb_ref[...])


def _merge(x2, branches, w_gate, b_gate, w_branch, w_out, ln_g, ln_b):
    n = x2.shape[0]
    tm = TM_MERGE
    nb = len(branches)

    def const2(shape):
        return pl.BlockSpec(shape, lambda i: (0, 0))

    return pl.pallas_call(
        _merge_kernel,
        grid=(n // tm,),
        in_specs=[pl.BlockSpec((tm, D_MODEL), lambda i: (i, 0))]
                 + [pl.BlockSpec((tm, BRANCH_WIDTH), lambda i: (i, 0))] * nb
                 + [const2((D_MODEL, nb * D_MODEL)), const2((1, nb * D_MODEL)),
                    pl.BlockSpec((nb, BRANCH_WIDTH, D_MODEL), lambda i: (0, 0, 0)),
                    const2((D_MODEL, D_MODEL)), const2((1, D_MODEL)), const2((1, D_MODEL))],
        out_specs=pl.BlockSpec((tm, D_MODEL), lambda i: (i, 0)),
        out_shape=jax.ShapeDtypeStruct((n, D_MODEL), F32),
        compiler_params=_params("parallel"),
        name="merge_out",
    )(x2, *branches, w_gate, b_gate[None, :], w_branch, w_out, ln_g[None, :], ln_b[None, :])


def _swiglu_partial(xb, wg, wu, wd):
    g = jnp.dot(xb, wg, preferred_element_type=F32)
    u = jnp.dot(xb, wu, preferred_element_type=F32)
    h = (g * _sigmoid(g) * u).astype(BF16)
    return jnp.dot(h, wd, preferred_element_type=F32)


def _dense_ffn_kernel(x_ref, wg_ref, wu_ref, wd_ref, lg_ref, lb_ref, o_ref, acc_ref):
    j = pl.program_id(1)
    part = _swiglu_partial(x_ref[...].astype(BF16), wg_ref[...], wu_ref[...], wd_ref[...])

    @pl.when(j == 0)
    def _():
        acc_ref[...] = part

    @pl.when(j > 0)
    def _():
        acc_ref[...] += part

    @pl.when(j == pl.num_programs(1) - 1)
    def _():
        o_ref[...] = _layer_norm_rows(DN_ALPHA * x_ref[...] + acc_ref[...], lg_ref[...], lb_ref[...])


def _dense_ffn(x2, wg, wu, wd, ln_g, ln_b):
    n = x2.shape[0]
    tm, tf = TM_FFN, TF_FFN
    return pl.pallas_call(
        _dense_ffn_kernel,
        grid=(n // tm, D_FF // tf),
        in_specs=[pl.BlockSpec((tm, D_MODEL), lambda i, j: (i, 0)),
                  pl.BlockSpec((D_MODEL, tf), lambda i, j: (0, j)),
                  pl.BlockSpec((D_MODEL, tf), lambda i, j: (0, j)),
                  pl.BlockSpec((tf, D_MODEL), lambda i, j: (j, 0)),
                  pl.BlockSpec((1, D_MODEL), lambda i, j: (0, 0)),
                  pl.BlockSpec((1, D_MODEL), lambda i, j: (0, 0))],
        out_specs=pl.BlockSpec((tm, D_MODEL), lambda i, j: (i, 0)),
        out_shape=jax.ShapeDtypeStruct((n, D_MODEL), F32),
        scratch_shapes=[pltpu.VMEM((tm, D_MODEL), F32)],
        compiler_params=_params("parallel", "arbitrary"),
        name="dense_ffn",
    )(x2, wg, wu, wd, ln_g[None, :], ln_b[None, :])


def _router_kernel(x_ref, wr_ref, tri_ref, meta_ref, cnt_ref, run_ref):
    @pl.when(pl.program_id(0) == 0)
    def _():
        run_ref[...] = jnp.zeros(run_ref.shape, F32)

    x = x_ref[...]
    w = wr_ref[...]
    x_hi = x.astype(BF16)
    x_lo = (x - x_hi.astype(F32)).astype(BF16)
    w_hi = w.astype(BF16)
    w_lo = (w - w_hi.astype(F32)).astype(BF16)
    logits = (jnp.dot(x_hi, w_hi, preferred_element_type=F32)
              + jnp.dot(x_lo, w_hi, preferred_element_type=F32)
              + jnp.dot(x_hi, w_lo, preferred_element_type=F32))
    lane = lax.broadcasted_iota(jnp.int32, logits.shape, 1)
    logits = jnp.where(lane < N_EXPERTS, logits, -jnp.inf)
    v1 = jnp.max(logits, axis=1, keepdims=True)
    i1 = jnp.min(jnp.where(logits == v1, lane, LANES), axis=1, keepdims=True)
    rest = jnp.where(lane == i1, -jnp.inf, logits)
    v2 = jnp.max(rest, axis=1, keepdims=True)
    i2 = jnp.min(jnp.where(rest == v2, lane, LANES), axis=1, keepdims=True)
    ex = jnp.exp(v2 - v1)
    den = 1.0 + ex
    w1 = 1.0 / den
    w2 = ex / den

    sel1 = lane == i1
    sel2 = lane == i2
    chosen = jnp.where(sel1 | sel2, 1.0, 0.0)
    run = run_ref[...]
    before = jnp.dot(tri_ref[...], chosen.astype(BF16), preferred_element_type=F32) + run
    r1 = jnp.sum(jnp.where(sel1, before, 0.0), axis=1, keepdims=True)
    r2 = jnp.sum(jnp.where(sel2, before, 0.0), axis=1, keepdims=True)
    run = run + jnp.sum(chosen, axis=0, keepdims=True)
    run_ref[...] = run
    cnt_ref[...] = run

    meta = jnp.where(lane == 0, i1.astype(F32), 0.0)
    meta = jnp.where(lane == 1, i2.astype(F32), meta)
    meta = jnp.where(lane == 2, w1, meta)
    meta = jnp.where(lane == 3, w2, meta)
    meta = jnp.where(lane == 4, r1, meta)
    meta = jnp.where(lane == 5, r2, meta)
    meta_ref[...] = meta


def _router(x2, w_router):
    n = x2.shape[0]
    tm = TM_ROUTE
    wr = jnp.pad(w_router, ((0, 0), (0, LANES - N_EXPERTS)))
    r = lax.broadcasted_iota(jnp.int32, (tm, tm), 0)
    c = lax.broadcasted_iota(jnp.int32, (tm, tm), 1)
    tri = (c < r).astype(BF16)
    return pl.pallas_call(
        _router_kernel,
        grid=(n // tm,),
        in_specs=[pl.BlockSpec((tm, D_MODEL), lambda i: (i, 0)),
                  pl.BlockSpec((D_MODEL, LANES), lambda i: (0, 0)),
                  pl.BlockSpec((tm, tm), lambda i: (0, 0))],
        out_specs=[pl.BlockSpec((tm, LANES), lambda i: (i, 0)),
                   pl.BlockSpec((1, LANES), lambda i: (0, 0))],
        out_shape=[jax.ShapeDtypeStruct((n, LANES), F32),
                   jax.ShapeDtypeStruct((1, LANES), F32)],
        scratch_shapes=[pltpu.VMEM((1, LANES), F32)],
        compiler_params=_params("arbitrary"),
        name="router",
    )(x2, wr, tri)


def _row_copy(src_hbm, dst_ref, src_row, dst_row, sem):
    return pltpu.make_async_copy(src_hbm.at[pl.ds(src_row, 1)], dst_ref.at[pl.ds(dst_row, 1)], sem)


def _gather_kernel(tok_ref, x_hbm, o_hbm, sem, *, g):
    base = pl.program_id(0) * g

    def start(r, c):
        _row_copy(x_hbm, o_hbm, tok_ref[0, 0, r], base + r, sem).start()
        return c

    lax.fori_loop(0, g, start, 0)

    def wait(r, c):
        _row_copy(x_hbm, o_hbm, 0, base + r, sem).wait()
        return c

    lax.fori_loop(0, g, wait, 0)


def _gather_rows(x2, tok):
    p = tok.shape[0]
    g = G_GATHER
    return pl.pallas_call(
        functools.partial(_gather_kernel, g=g),
        grid=(p // g,),
        in_specs=[pl.BlockSpec((1, 1, g), lambda i: (i, 0, 0), memory_space=pltpu.SMEM),
                  pl.BlockSpec(memory_space=pl.ANY)],
        out_specs=pl.BlockSpec(memory_space=pl.ANY),
        out_shape=jax.ShapeDtypeStruct((p, D_MODEL), x2.dtype),
        scratch_shapes=[pltpu.SemaphoreType.DMA],
        compiler_params=_params("arbitrary"),
        name="moe_gather",
    )(tok.reshape(p // g, 1, g), x2)


def _moe_ffn_kernel(te_ref, nt_ref, x_ref, wg_ref, wu_ref, wd_ref, o_ref, acc_ref):
    i = pl.program_id(0)
    j = pl.program_id(1)

    @pl.when(i < nt_ref[0])
    def _():
        part = _swiglu_partial(x_ref[...].astype(BF16), wg_ref[0], wu_ref[0], wd_ref[0])

        @pl.when(j == 0)
        def _():
            acc_ref[...] = part

        @pl.when(j > 0)
        def _():
            acc_ref[...] += part

        @pl.when(j == pl.num_programs(1) - 1)
        def _():
            o_ref[...] = acc_ref[...]

    @pl.when(i >= nt_ref[0])
    def _():
        o_ref[...] = jnp.zeros(o_ref.shape, o_ref.dtype)


def _moe_ffn(xs, tile_expert, num_tiles, wg, wu, wd):
    p = xs.shape[0]
    tm, tf = TM_FFN, TF_FFN
    nj = D_FF // tf

    def row(i, j, te, nt):
        return (jnp.minimum(i, nt[0] - 1), 0)

    def col_j(i, j, nt):
        return jnp.where(i < nt[0], j, nj - 1)

    def w_up(i, j, te, nt):
        return (te[jnp.minimum(i, nt[0] - 1)], 0, col_j(i, j, nt))

    def w_down(i, j, te, nt):
        return (te[jnp.minimum(i, nt[0] - 1)], col_j(i, j, nt), 0)

    grid_spec = pltpu.PrefetchScalarGridSpec(
        num_scalar_prefetch=2,
        grid=(p // tm, nj),
        in_specs=[pl.BlockSpec((tm, D_MODEL), row),
                  pl.BlockSpec((1, D_MODEL, tf), w_up),
                  pl.BlockSpec((1, D_MODEL, tf), w_up),
                  pl.BlockSpec((1, tf, D_MODEL), w_down)],
        out_specs=pl.BlockSpec((tm, D_MODEL), lambda i, j, te, nt: (i, 0)),
        scratch_shapes=[pltpu.VMEM((tm, D_MODEL), F32)],
    )
    return pl.pallas_call(
        _moe_ffn_kernel,
        grid_spec=grid_spec,
        out_shape=jax.ShapeDtypeStruct((p, D_MODEL), F32),
        compiler_params=_params("arbitrary", "arbitrary"),
        name="moe_ffn",
    )(tile_expert, num_tiles, xs, wg, wu, wd)


def _combine_kernel(p1_ref, p2_ref, x_ref, w1_ref, w2_ref, y_hbm, lg_ref, lb_ref, o_ref,
                    buf1, buf2, sem, *, tc):
    def start(r, c):
        _row_copy(y_hbm, buf1, p1_ref[0, 0, r], r, sem.at[0]).start()
        _row_copy(y_hbm, buf2, p2_ref[0, 0, r], r, sem.at[1]).start()
        return c

    lax.fori_loop(0, tc, start, 0)

    def wait(r, c):
        _row_copy(y_hbm, buf1, 0, r, sem.at[0]).wait()
        _row_copy(y_hbm, buf2, 0, r, sem.at[1]).wait()
        return c

    lax.fori_loop(0, tc, wait, 0)
    f = w1_ref[...] * buf1[...] + w2_ref[...] * buf2[...]
    o_ref[...] = _layer_norm_rows(DN_ALPHA * x_ref[...] + f, lg_ref[...], lb_ref[...])


def _combine(x2, y, pos1, pos2, w1, w2, ln_g, ln_b):
    n = x2.shape[0]
    tc = TC_COMBINE
    idx = pl.BlockSpec((1, 1, tc), lambda i: (i, 0, 0), memory_space=pltpu.SMEM)
    return pl.pallas_call(
        functools.partial(_combine_kernel, tc=tc),
        grid=(n // tc,),
        in_specs=[idx, idx,
                  pl.BlockSpec((tc, D_MODEL), lambda i: (i, 0)),
                  pl.BlockSpec((tc, 1), lambda i: (i, 0)),
                  pl.BlockSpec((tc, 1), lambda i: (i, 0)),
                  pl.BlockSpec(memory_space=pl.ANY),
                  pl.BlockSpec((1, D_MODEL), lambda i: (0, 0)),
                  pl.BlockSpec((1, D_MODEL), lambda i: (0, 0))],
        out_specs=pl.BlockSpec((tc, D_MODEL), lambda i: (i, 0)),
        out_shape=jax.ShapeDtypeStruct((n, D_MODEL), F32),
        scratch_shapes=[pltpu.VMEM((tc, D_MODEL), F32), pltpu.VMEM((tc, D_MODEL), F32),
                        pltpu.SemaphoreType.DMA((2,))],
        compiler_params=_params("arbitrary"),
        name="moe_combine",
    )(pos1.reshape(n // tc, 1, tc), pos2.reshape(n // tc, 1, tc), x2, w1, w2, y,
      ln_g[None, :], ln_b[None, :])


def _moe_layer(x2, w_router, wg, wu, wd, ln_g, ln_b):
    n = x2.shape[0]
    tm = TM_FFN
    n_tiles = (2 * n) // tm + N_EXPERTS
    p = n_tiles * tm

    meta, counts = _router(x2, w_router)
    e1 = meta[:, 0].astype(jnp.int32)
    e2 = meta[:, 1].astype(jnp.int32)
    w1 = meta[:, 2:3]
    w2 = meta[:, 3:4]
    r1 = meta[:, 4].astype(jnp.int32)
    r2 = meta[:, 5].astype(jnp.int32)
    cnt = counts[0, :N_EXPERTS].astype(jnp.int32)

    tiles_per = (cnt + tm - 1) // tm
    tile_end = jnp.cumsum(tiles_per)
    start = (tile_end - tiles_per) * tm
    pos1 = start[e1] + r1
    pos2 = start[e2] + r2
    num_tiles = tile_end[-1:]
    tile_expert = jnp.minimum(
        jnp.sum((jnp.arange(n_tiles)[:, None] >= tile_end[None, :]).astype(jnp.int32), axis=1),
        N_EXPERTS - 1)
    token = jnp.arange(n, dtype=jnp.int32)
    tok = jnp.zeros((p,), jnp.int32).at[pos1].set(token).at[pos2].set(token)

    xs = _gather_rows(x2, tok)
    y = _moe_ffn(xs, tile_expert, num_tiles, wg, wu, wd)
    return _combine(x2, y, pos1, pos2, w1, w2, ln_g, ln_b)


def _prep_in_proj(w):
    sc = w[:, 0:1536]
    cf = w[:, 1536:2560]
    cq = w[:, 2560:2816]
    ckv = w[:, 2816:2944]
    kr = w[:, 2944:2976]
    sb = w[:, 2976:4512]
    gate = w[:, 4512:]
    half = MLA_ROPE // 2
    z_nope = jnp.zeros((D_MODEL, MLA_NOPE), F32)
    z_tail = jnp.zeros((D_MODEL, HEAD_PAD - MLA_NOPE - MLA_ROPE), F32)
    kr_pad = jnp.concatenate([z_nope, kr, z_tail], axis=1)
    kr_rot = jnp.concatenate([z_nope, -kr[:, half:], kr[:, :half], z_tail], axis=1)
    hw = SB_HEADS * SB_HEAD_DIM
    sbq = sb[:, :hw] * (-(SB_HEAD_DIM ** -0.5))
    mix = jnp.concatenate([sc, cf, cq, ckv, kr_pad, kr_rot, sbq, sb[:, hw:]], axis=1)
    return mix.astype(BF16), gate.astype(BF16)


def _prep_mla(w_uq, w_ukv):
    half = MLA_ROPE // 2
    wq = w_uq.reshape(MLA_Q_RANK, MLA_HEADS, MLA_NOPE + MLA_ROPE)
    nope, rope = wq[..., :MLA_NOPE], wq[..., MLA_NOPE:]
    tail = jnp.zeros((MLA_Q_RANK, MLA_HEADS, HEAD_PAD - MLA_NOPE - MLA_ROPE), F32)
    q_main = jnp.concatenate([nope, rope, tail], axis=-1)
    q_rot = jnp.concatenate([jnp.zeros_like(nope), -rope[..., half:], rope[..., :half], tail], axis=-1)
    wkv = w_ukv.reshape(MLA_KV_RANK, MLA_HEADS, MLA_NOPE + MLA_V)
    k_nope, v = wkv[..., :MLA_NOPE], wkv[..., MLA_NOPE:]
    k_main = jnp.concatenate(
        [k_nope, jnp.zeros((MLA_KV_RANK, MLA_HEADS, HEAD_PAD - MLA_NOPE), F32)], axis=-1)
    hw = MLA_HEADS * HEAD_PAD
    return (q_main.reshape(MLA_Q_RANK, hw).astype(BF16), q_rot.reshape(MLA_Q_RANK, hw).astype(BF16),
            k_main.reshape(MLA_KV_RANK, hw).astype(BF16),
            v.reshape(MLA_KV_RANK, MLA_HEADS * MLA_V).astype(BF16))


def kernel(x, positions, w_in, b_gate, sc_conv, cf_conv, cf_conv_bias, cf_ln_g, cf_ln_b,
           mla_q_norm, mla_w_uq, mla_kv_norm, mla_w_ukv, w_branch, w_out,
           ln_mix_g, ln_mix_b, ln_ffn_g, ln_ffn_b, ffn_w_gate, ffn_w_up, ffn_w_down,
           router_w, exp_w_gate, exp_w_up, exp_w_down):
    bsz, seq, d = x.shape
    n = bsz * seq
    cos, sin = _rope_tables(positions)
    x2 = x.reshape(n, d)
    for layer in range(DEPTH):
        w_mix, w_gate = _prep_in_proj(w_in[layer])
        wq, wqr, wk, wv = _prep_mla(mla_w_uq[layer], mla_w_ukv[layer])
        proj = _inproj(x2, w_mix).reshape(bsz, seq, N_MIX)
        br0, br1 = _conv_branches(proj, sc_conv[layer], cf_conv[layer], cf_conv_bias[layer],
                                  cf_ln_g[layer], cf_ln_b[layer])
        q, k, v = _mla_prep(proj, cos, sin, mla_q_norm[layer], mla_kv_norm[layer], wq, wqr, wk, wv)
        br2 = _mla_attention(q, k, v)
        br3 = _sb_attention(proj)
        branches = [b.reshape(n, BRANCH_WIDTH) for b in (br0, br1, br2, br3)]
        x2 = _merge(x2, branches, w_gate, b_gate[layer], w_branch[layer].astype(BF16),
                    w_out[layer].astype(BF16), ln_mix_g[layer], ln_mix_b[layer])
        i = layer // 2
        if layer % 2 == 0:
            x2 = _dense_ffn(x2, ffn_w_gate[i].astype(BF16), ffn_w_up[i].astype(BF16),
                            ffn_w_down[i].astype(BF16), ln_ffn_g[layer], ln_ffn_b[layer])
        else:
            x2 = _moe_layer(x2, router_w[i], exp_w_gate[i].astype(BF16), exp_w_up[i].astype(BF16),
                            exp_w_down[i].astype(BF16), ln_ffn_g[layer], ln_ffn_b[layer])
    return x2.reshape(bsz, seq, d)
```

```python
import functools
import math

import jax
import jax.numpy as jnp
from jax import lax
from jax.experimental import pallas as pl
from jax.experimental.pallas import tpu as pltpu

F32 = jnp.float32
BF16 = jnp.bfloat16

D_MODEL = 1024
DEPTH = 2
CHUNK = 64
BRANCH_WIDTH = 512
SC_CONV_LEN = 3
CF_CONV_LEN = 31
MLA_HEADS = 8
MLA_NOPE = 64
MLA_ROPE = 32
MLA_V = 64
MLA_Q_RANK = 256
MLA_KV_RANK = 128
ROPE_THETA = 10000.0
SB_HEADS = 8
SB_HEAD_DIM = 64
D_FF = 3584
N_EXPERTS = 8
DN_ALPHA = (2 * DEPTH) ** 0.25
LN_EPS = 1e-5
RMS_EPS = 1e-6

LANES = 128
HEAD_PAD = 128
CONV_HALO = 32
VMEM_LIMIT = 56 * 1024 * 1024

OFF_SCB, OFF_SCC, OFF_SCH = 0, 512, 1024
OFF_CFV, OFF_CFG = 1536, 2048
OFF_CQ = 2560
OFF_CKV = 2816
OFF_KR = 2944
OFF_KRR = 3072
OFF_SBQ, OFF_SBK, OFF_SBV = 3200, 3712, 4224
N_MIX = 4736

TM_PROJ = 512
TS_CONV = 512
TS_PREP = 512
T_ATT = 512
TM_MERGE = 512
TM_FFN = 512
TF_FFN = 1792
TM_ROUTE = 512
G_GATHER = 512
TC_COMBINE = 256
DMA_UNROLL = 8

_NT = (((1,), (1,)), ((), ()))


def _params(*sem):
    return pltpu.CompilerParams(dimension_semantics=sem, vmem_limit_bytes=VMEM_LIMIT)


def _sigmoid(x):
    return 1.0 / (1.0 + jnp.exp(-x))


def _layer_norm_rows(v, g, b):
    mu = jnp.mean(v, axis=-1, keepdims=True)
    c = v - mu
    var = jnp.mean(c * c, axis=-1, keepdims=True)
    return c * lax.rsqrt(var + LN_EPS) * g + b


def _rope_kernel(pos_ref, freq_ref, cos_ref, sin_ref):
    ang = pos_ref[0] * freq_ref[...]
    cos_ref[0] = jnp.cos(ang)
    sin_ref[0] = jnp.sin(ang)


def _rope_tables(positions):
    bsz, seq = positions.shape
    half = MLA_ROPE // 2
    inv_freq = 1.0 / (ROPE_THETA ** (jnp.arange(half, dtype=F32) * (2.0 / MLA_ROPE)))
    freq = jnp.concatenate([jnp.zeros((MLA_NOPE,), F32), inv_freq, inv_freq,
                            jnp.zeros((HEAD_PAD - MLA_NOPE - MLA_ROPE,), F32)])[None, :]
    pos = positions.astype(F32)[..., None]
    ts = TS_PREP
    return pl.pallas_call(
        _rope_kernel,
        grid=(bsz, seq // ts),
        in_specs=[pl.BlockSpec((1, ts, 1), lambda b, i: (b, i, 0)),
                  pl.BlockSpec((1, HEAD_PAD), lambda b, i: (0, 0))],
        out_specs=[pl.BlockSpec((1, ts, HEAD_PAD), lambda b, i: (b, i, 0))] * 2,
        out_shape=[jax.ShapeDtypeStruct((bsz, seq, HEAD_PAD), F32)] * 2,
        compiler_params=_params("parallel", "parallel"),
        name="rope_tables",
    )(pos, freq)


def _inproj_kernel(x_ref, w_ref, o_ref):
    xb = x_ref[...].astype(BF16)
    for c0 in range(0, N_MIX, 512):
        c1 = min(c0 + 512, N_MIX)
        o_ref[:, c0:c1] = jnp.dot(xb, w_ref[:, c0:c1], preferred_element_type=F32).astype(BF16)


def _inproj(x2, w_mix):
    n = x2.shape[0]
    tm = TM_PROJ
    return pl.pallas_call(
        _inproj_kernel,
        grid=(n // tm,),
        in_specs=[pl.BlockSpec((tm, D_MODEL), lambda i: (i, 0)),
                  pl.BlockSpec((D_MODEL, N_MIX), lambda i: (0, 0))],
        out_specs=pl.BlockSpec((tm, N_MIX), lambda i: (i, 0)),
        out_shape=jax.ShapeDtypeStruct((n, N_MIX), BF16),
        compiler_params=_params("parallel"),
        name="in_proj",
    )(x2, w_mix)


def _conv_kernel(b_ref, c_ref, h_ref, v_ref, g_ref, ch_ref, hh_ref, vh_ref, gh_ref,
                 wsc_ref, wcf_ref, bcf_ref, lng_ref, lnb_ref, o0_ref, o1_ref, buf, *, ts):
    has_history = pl.program_id(1) > 0
    halo = CONV_HALO

    hist = ch_ref[0].astype(F32) * hh_ref[0].astype(F32)
    buf[0:halo, :] = jnp.where(has_history, hist, 0.0)
    buf[halo:halo + ts, :] = c_ref[0].astype(F32) * h_ref[0].astype(F32)
    acc = None
    for l in range(SC_CONV_LEN):
        off = halo - (SC_CONV_LEN - 1) + l
        term = wsc_ref[l:l + 1, :] * buf[off:off + ts, :]
        acc = term if acc is None else acc + term
    o0_ref[0] = (b_ref[0].astype(F32) * acc).astype(BF16)

    hist = vh_ref[0].astype(F32) * _sigmoid(gh_ref[0].astype(F32))
    buf[0:halo, :] = jnp.where(has_history, hist, 0.0)
    buf[halo:halo + ts, :] = v_ref[0].astype(F32) * _sigmoid(g_ref[0].astype(F32))
    acc = None
    for l in range(CF_CONV_LEN):
        off = halo - (CF_CONV_LEN - 1) + l
        term = wcf_ref[l:l + 1, :] * buf[off:off + ts, :]
        acc = term if acc is None else acc + term
    y = _layer_norm_rows(acc + bcf_ref[...], lng_ref[...], lnb_ref[...])
    o1_ref[0] = (y * _sigmoid(y)).astype(BF16)


def _conv_branches(proj3, sc_conv, cf_conv, cf_bias, cf_ln_g, cf_ln_b):
    bsz, seq, _ = proj3.shape
    ts = TS_CONV
    w = BRANCH_WIDTH
    hb = ts // CONV_HALO

    def main(col):
        return pl.BlockSpec((1, ts, w), lambda b, i: (b, i, col))

    def halo(col):
        return pl.BlockSpec((1, CONV_HALO, w), lambda b, i: (b, jnp.maximum(i * hb - 1, 0), col))

    def vec(rows):
        return pl.BlockSpec((rows, w), lambda b, i: (0, 0))

    return pl.pallas_call(
        functools.partial(_conv_kernel, ts=ts),
        grid=(bsz, seq // ts),
        in_specs=[main(OFF_SCB // w), main(OFF_SCC // w), main(OFF_SCH // w),
                  main(OFF_CFV // w), main(OFF_CFG // w),
                  halo(OFF_SCC // w), halo(OFF_SCH // w), halo(OFF_CFV // w), halo(OFF_CFG // w),
                  vec(SC_CONV_LEN), vec(CF_CONV_LEN), vec(1), vec(1), vec(1)],
        out_specs=[pl.BlockSpec((1, ts, w), lambda b, i: (b, i, 0))] * 2,
        out_shape=[jax.ShapeDtypeStruct((bsz, seq, w), BF16)] * 2,
        scratch_shapes=[pltpu.VMEM((CONV_HALO + ts, w), F32)],
        compiler_params=_params("parallel", "parallel"),
        name="conv_branches",
    )(proj3, proj3, proj3, proj3, proj3, proj3, proj3, proj3, proj3,
      sc_conv, cf_conv, cf_bias[None, :], cf_ln_g[None, :], cf_ln_b[None, :])


def _mla_prep_kernel(cq_ref, ckv_ref, kr_ref, krr_ref, cos_ref, sin_ref, qn_ref, kvn_ref,
                     wq_ref, wqr_ref, wk_ref, wv_ref, q_ref, k_ref, v_ref):
    cos = cos_ref[0]
    sin = sin_ref[0]
    cos8 = jnp.concatenate([cos] * MLA_HEADS, axis=1)
    sin8 = jnp.concatenate([sin] * MLA_HEADS, axis=1)
    scale = (MLA_NOPE + MLA_ROPE) ** -0.5 * math.log2(math.e)

    cq = cq_ref[0].astype(F32)
    nq = cq * lax.rsqrt(jnp.mean(cq * cq, axis=-1, keepdims=True) + RMS_EPS) * qn_ref[...]
    nq = nq.astype(BF16)
    q1 = jnp.dot(nq, wq_ref[...], preferred_element_type=F32)
    q2 = jnp.dot(nq, wqr_ref[...], preferred_element_type=F32)
    q_ref[0] = ((q1 * cos8 + q2 * sin8) * scale).astype(BF16)

    ckv = ckv_ref[0].astype(F32)
    nkv = ckv * lax.rsqrt(jnp.mean(ckv * ckv, axis=-1, keepdims=True) + RMS_EPS) * kvn_ref[...]
    nkv = nkv.astype(BF16)
    k_rope = kr_ref[0].astype(F32) * cos + krr_ref[0].astype(F32) * sin
    k_nope = jnp.dot(nkv, wk_ref[...], preferred_element_type=F32)
    k_ref[0] = (k_nope + jnp.concatenate([k_rope] * MLA_HEADS, axis=1)).astype(BF16)
    v_ref[0] = jnp.dot(nkv, wv_ref[...], preferred_element_type=F32).astype(BF16)


def _mla_prep(proj3, cos, sin, q_norm, kv_norm, wq, wqr, wk, wv):
    bsz, seq, _ = proj3.shape
    ts = TS_PREP
    hw = MLA_HEADS * HEAD_PAD
    vw = MLA_HEADS * MLA_V

    def const(shape):
        return pl.BlockSpec(shape, lambda b, i: (0, 0))

    return pl.pallas_call(
        _mla_prep_kernel,
        grid=(bsz, seq // ts),
        in_specs=[pl.BlockSpec((1, ts, MLA_Q_RANK), lambda b, i: (b, i, OFF_CQ // MLA_Q_RANK)),
                  pl.BlockSpec((1, ts, MLA_KV_RANK), lambda b, i: (b, i, OFF_CKV // MLA_KV_RANK)),
                  pl.BlockSpec((1, ts, HEAD_PAD), lambda b, i: (b, i, OFF_KR // HEAD_PAD)),
                  pl.BlockSpec((1, ts, HEAD_PAD), lambda b, i: (b, i, OFF_KRR // HEAD_PAD)),
                  pl.BlockSpec((1, ts, HEAD_PAD), lambda b, i: (b, i, 0)),
                  pl.BlockSpec((1, ts, HEAD_PAD), lambda b, i: (b, i, 0)),
                  const((1, MLA_Q_RANK)), const((1, MLA_KV_RANK)),
                  const((MLA_Q_RANK, hw)), const((MLA_Q_RANK, hw)),
                  const((MLA_KV_RANK, hw)), const((MLA_KV_RANK, vw))],
        out_specs=[pl.BlockSpec((1, ts, hw), lambda b, i: (b, i, 0)),
                   pl.BlockSpec((1, ts, hw), lambda b, i: (b, i, 0)),
                   pl.BlockSpec((1, ts, vw), lambda b, i: (b, i, 0))],
        out_shape=[jax.ShapeDtypeStruct((bsz, seq, hw), BF16),
                   jax.ShapeDtypeStruct((bsz, seq, hw), BF16),
                   jax.ShapeDtypeStruct((bsz, seq, vw), BF16)],
        compiler_params=_params("parallel", "parallel"),
        name="mla_prep",
    )(proj3, proj3, proj3, proj3, cos, sin, q_norm[None, :], kv_norm[None, :], wq, wqr, wk, wv)


def _mla_attn_kernel(q_ref, k_ref, v_ref, o_ref, m_ref, acc_ref, *, t):
    qi = pl.program_id(2)
    row = lax.broadcasted_iota(jnp.int32, (t, t), 0)
    col = lax.broadcasted_iota(jnp.int32, (t, t), 1)
    allowed = (col // CHUNK) <= (row // CHUNK)
    ones = jnp.ones((t, LANES), BF16)
    m_ref[...] = jnp.full(m_ref.shape, -jnp.inf, F32)
    acc_ref[...] = jnp.zeros(acc_ref.shape, F32)

    def block(kstart, masked):
        v_ext = jnp.concatenate([v_ref[0, pl.ds(kstart, t), :], ones], axis=1)
        for hh in range(2):
            lo = hh * HEAD_PAD
            qh = q_ref[0, :, lo:lo + HEAD_PAD]
            kblk = k_ref[0, pl.ds(kstart, t), lo:lo + HEAD_PAD]
            s = lax.dot_general(qh, kblk, _NT, preferred_element_type=F32)
            if masked:
                s = jnp.where(allowed, s, -jnp.inf)
            m_prev = m_ref[hh]
            m_new = jnp.maximum(m_prev, jnp.max(s, axis=1, keepdims=True))
            alpha = jnp.exp2(m_prev - m_new)
            p = jnp.exp2(s - jnp.concatenate([m_new] * (t // LANES), axis=1))
            acc_ref[hh] = (jnp.concatenate([alpha, alpha], axis=1) * acc_ref[hh]
                           + jnp.dot(p.astype(BF16), v_ext, preferred_element_type=F32))
            m_ref[hh] = m_new

    def pair(i, carry):
        block(pl.multiple_of(2 * i * t, t), False)
        block(pl.multiple_of((2 * i + 1) * t, t), False)
        return carry

    lax.fori_loop(0, lax.shift_right_logical(qi, 1), pair, 0)

    @pl.when((qi & 1) == 1)
    def _():
        block(pl.multiple_of((qi - 1) * t, t), False)

    block(pl.multiple_of(qi * t, t), True)
    pv = 2 * MLA_V
    lane = lax.broadcasted_iota(jnp.int32, (t, pv), 1)
    out0 = acc_ref[0, :, :pv] / acc_ref[0, :, pv:]
    out1 = acc_ref[1, :, :pv] / acc_ref[1, :, pv:]
    o_ref[0] = jnp.where(lane < MLA_V, out0, out1).astype(BF16)


def _mla_attention(q, k, v):
    bsz, seq, _ = q.shape
    t = T_ATT
    pairs = MLA_HEADS // 2
    pv = 2 * MLA_V
    return pl.pallas_call(
        functools.partial(_mla_attn_kernel, t=t),
        grid=(bsz, pairs, seq // t),
        in_specs=[pl.BlockSpec((1, t, 2 * HEAD_PAD), lambda b, h, i: (b, i, h)),
                  pl.BlockSpec((1, seq, 2 * HEAD_PAD), lambda b, h, i: (b, 0, h)),
                  pl.BlockSpec((1, seq, pv), lambda b, h, i: (b, 0, h))],
        out_specs=pl.BlockSpec((1, t, pv), lambda b, h, i: (b, i, h)),
        out_shape=jax.ShapeDtypeStruct((bsz, seq, MLA_HEADS * MLA_V), BF16),
        scratch_shapes=[pltpu.VMEM((2, t, LANES), F32), pltpu.VMEM((2, t, pv + LANES), F32)],
        compiler_params=_params("parallel", "parallel", "arbitrary"),
        name="mla_attention",
    )(q, k, v)


def _sb_attn_kernel(q_ref, k_ref, v_ref, u_ref, o_ref, acc_ref, carry_ref, *, t):
    qi = pl.program_id(2)
    row = lax.broadcasted_iota(jnp.int32, (t, t), 0)
    col = lax.broadcasted_iota(jnp.int32, (t, t), 1)
    before = col < row
    lane = lax.broadcasted_iota(jnp.int32, (t, 2 * SB_HEAD_DIM), 1)
    half = t // 2
    reps = half // LANES
    acc_ref[...] = jnp.zeros(acc_ref.shape, F32)
    carry_ref[...] = jnp.zeros(carry_ref.shape, F32)

    def block(kstart, masked):
        kblk = k_ref[0, pl.ds(kstart, t), :]
        vblk = v_ref[0, pl.ds(kstart, t), :]
        q = q_ref[0]
        for hh in range(2):
            in_head = (lane >= hh * SB_HEAD_DIM) & (lane < (hh + 1) * SB_HEAD_DIM)
            qm = jnp.where(in_head, q, jnp.zeros_like(q))
            nz = lax.dot_general(qm, kblk, _NT, preferred_element_type=F32)
            log_stay = jnp.minimum(nz, 0.0) - jnp.log2(1.0 + jnp.exp2(-jnp.abs(nz)))
            if masked:
                log_stay = jnp.where(before, log_stay, 0.0)
            stay_lo, stay_hi = log_stay[:, :half], log_stay[:, half:]
            sums_hi = jnp.dot(stay_hi.astype(BF16), u_ref[...], preferred_element_type=F32)
            sums_lo = jnp.dot(stay_lo.astype(BF16), u_ref[...], preferred_element_type=F32)
            carry_hi = carry_ref[hh]
            carry_lo = carry_hi + jnp.sum(stay_hi, axis=1, keepdims=True)
            arg_hi = sums_hi + (jnp.concatenate([carry_hi] * reps, axis=1) - nz[:, half:])
            arg_lo = sums_lo + (jnp.concatenate([carry_lo] * reps, axis=1) - nz[:, :half])
            a = jnp.exp2(jnp.concatenate([arg_lo, arg_hi], axis=1))
            if masked:
                a = jnp.where(before, a, 0.0)
            acc_ref[hh] += jnp.dot(a.astype(BF16), vblk, preferred_element_type=F32)
            carry_ref[hh] = carry_lo + jnp.sum(stay_lo, axis=1, keepdims=True)

    block(pl.multiple_of(qi * t, t), True)

    def pair(i, c):
        block(pl.multiple_of((qi - 1 - 2 * i) * t, t), False)
        block(pl.multiple_of((qi - 2 - 2 * i) * t, t), False)
        return c

    lax.fori_loop(0, lax.shift_right_logical(qi, 1), pair, 0)

    @pl.when((qi & 1) == 1)
    def _():
        block(0, False)

    o_ref[0] = jnp.where(lane < SB_HEAD_DIM, acc_ref[0], acc_ref[1]).astype(BF16)


def _sb_attention(proj3):
    bsz, seq, _ = proj3.shape
    t = T_ATT
    half = t // 2
    pairs = SB_HEADS // 2
    pw = 2 * SB_HEAD_DIM
    r = lax.broadcasted_iota(jnp.int32, (half, half), 0)
    c = lax.broadcasted_iota(jnp.int32, (half, half), 1)
    suffix = (r >= c).astype(BF16)
    return pl.pallas_call(
        functools.partial(_sb_attn_kernel, t=t),
        grid=(bsz, pairs, seq // t),
        in_specs=[pl.BlockSpec((1, t, pw), lambda b, h, i: (b, i, OFF_SBQ // pw + h)),
                  pl.BlockSpec((1, seq, pw), lambda b, h, i: (b, 0, OFF_SBK // pw + h)),
                  pl.BlockSpec((1, seq, pw), lambda b, h, i: (b, 0, OFF_SBV // pw + h)),
                  pl.BlockSpec((half, half), lambda b, h, i: (0, 0))],
        out_specs=pl.BlockSpec((1, t, pw), lambda b, h, i: (b, i, h)),
        out_shape=jax.ShapeDtypeStruct((bsz, seq, SB_HEADS * SB_HEAD_DIM), BF16),
        scratch_shapes=[pltpu.VMEM((2, t, pw), F32), pltpu.VMEM((2, t, LANES), F32)],
        compiler_params=_params("parallel", "parallel", "arbitrary"),
        name="sb_attention",
    )(proj3, proj3, proj3, suffix)


def _merge_kernel(x_ref, b0_ref, b1_ref, b2_ref, b3_ref, wg_ref, bg_ref, wb_ref, wo_ref,
                  lg_ref, lb_ref, o_ref):
    x = x_ref[...]
    xb = x.astype(BF16)
    merged = None
    for n, br in enumerate((b0_ref, b1_ref, b2_ref, b3_ref)):
        c0 = n * D_MODEL
        pre = jnp.dot(xb, wg_ref[:, c0:c0 + D_MODEL], preferred_element_type=F32)
        gate = _sigmoid(pre + bg_ref[:, c0:c0 + D_MODEL])
        term = gate * jnp.dot(br[...], wb_ref[n], preferred_element_type=F32)
        merged = term if merged is None else merged + term
    y = jnp.dot(merged.astype(BF16), wo_ref[...], preferred_element_type=F32)
    o_ref[...] = _layer_norm_rows(DN_ALPHA * x + y, lg_ref[...], lb_ref[...])


def _merge(x2, branches, w_gate, b_gate, w_branch, w_out, ln_g, ln_b):
    n = x2.shape[0]
    tm = TM_MERGE
    nb = len(branches)

    def const2(shape):
        return pl.BlockSpec(shape, lambda i: (0, 0))

    return pl.pallas_call(
        _merge_kernel,
        grid=(n // tm,),
        in_specs=[pl.BlockSpec((tm, D_MODEL), lambda i: (i, 0))]
                 + [pl.BlockSpec((tm, BRANCH_WIDTH), lambda i: (i, 0))] * nb
                 + [const2((D_MODEL, nb * D_MODEL)), const2((1, nb * D_MODEL)),
                    pl.BlockSpec((nb, BRANCH_WIDTH, D_MODEL), lambda i: (0, 0, 0)),
                    const2((D_MODEL, D_MODEL)), const2((1, D_MODEL)), const2((1, D_MODEL))],
        out_specs=pl.BlockSpec((tm, D_MODEL), lambda i: (i, 0)),
        out_shape=jax.ShapeDtypeStruct((n, D_MODEL), F32),
        compiler_params=_params("parallel"),
        name="merge_out",
    )(x2, *branches, w_gate, b_gate[None, :], w_branch, w_out, ln_g[None, :], ln_b[None, :])


def _swiglu_partial(xb, wg, wu, wd):
    g = jnp.dot(xb, wg, preferred_element_type=F32)
    u = jnp.dot(xb, wu, preferred_element_type=F32)
    h = (g * _sigmoid(g) * u).astype(BF16)
    return jnp.dot(h, wd, preferred_element_type=F32)


def _dense_ffn_kernel(x_ref, wg_ref, wu_ref, wd_ref, lg_ref, lb_ref, o_ref, acc_ref):
    j = pl.program_id(1)
    part = _swiglu_partial(x_ref[...].astype(BF16), wg_ref[...], wu_ref[...], wd_ref[...])

    @pl.when(j == 0)
    def _():
        acc_ref[...] = part

    @pl.when(j > 0)
    def _():
        acc_ref[...] += part

    @pl.when(j == pl.num_programs(1) - 1)
    def _():
        o_ref[...] = _layer_norm_rows(DN_ALPHA * x_ref[...] + acc_ref[...], lg_ref[...], lb_ref[...])


def _dense_ffn(x2, wg, wu, wd, ln_g, ln_b):
    n = x2.shape[0]
    tm, tf = TM_FFN, TF_FFN
    return pl.pallas_call(
        _dense_ffn_kernel,
        grid=(n // tm, D_FF // tf),
        in_specs=[pl.BlockSpec((tm, D_MODEL), lambda i, j: (i, 0)),
                  pl.BlockSpec((D_MODEL, tf), lambda i, j: (0, j)),
                  pl.BlockSpec((D_MODEL, tf), lambda i, j: (0, j)),
                  pl.BlockSpec((tf, D_MODEL), lambda i, j: (j, 0)),
                  pl.BlockSpec((1, D_MODEL), lambda i, j: (0, 0)),
                  pl.BlockSpec((1, D_MODEL), lambda i, j: (0, 0))],
        out_specs=pl.BlockSpec((tm, D_MODEL), lambda i, j: (i, 0)),
        out_shape=jax.ShapeDtypeStruct((n, D_MODEL), F32),
        scratch_shapes=[pltpu.VMEM((tm, D_MODEL), F32)],
        compiler_params=_params("parallel", "arbitrary"),
        name="dense_ffn",
    )(x2, wg, wu, wd, ln_g[None, :], ln_b[None, :])


def _router_kernel(x_ref, wr_ref, tri_ref, meta_ref, cnt_ref, run_ref):
    @pl.when(pl.program_id(0) == 0)
    def _():
        run_ref[...] = jnp.zeros(run_ref.shape, F32)

    x = x_ref[...]
    w = wr_ref[...]
    x_hi = x.astype(BF16)
    x_lo = (x - x_hi.astype(F32)).astype(BF16)
    w_hi = w.astype(BF16)
    w_lo = (w - w_hi.astype(F32)).astype(BF16)
    logits = (jnp.dot(x_hi, w_hi, preferred_element_type=F32)
              + jnp.dot(x_lo, w_hi, preferred_element_type=F32)
              + jnp.dot(x_hi, w_lo, preferred_element_type=F32))
    lane = lax.broadcasted_iota(jnp.int32, logits.shape, 1)
    logits = jnp.where(lane < N_EXPERTS, logits, -jnp.inf)
    v1 = jnp.max(logits, axis=1, keepdims=True)
    i1 = jnp.min(jnp.where(logits == v1, lane, LANES), axis=1, keepdims=True)
    rest = jnp.where(lane == i1, -jnp.inf, logits)
    v2 = jnp.max(rest, axis=1, keepdims=True)
    i2 = jnp.min(jnp.where(rest == v2, lane, LANES), axis=1, keepdims=True)
    ex = jnp.exp(v2 - v1)
    den = 1.0 + ex
    w1 = 1.0 / den
    w2 = ex / den

    sel1 = lane == i1
    sel2 = lane == i2
    chosen = jnp.where(sel1 | sel2, 1.0, 0.0)
    run = run_ref[...]
    before = jnp.dot(tri_ref[...], chosen.astype(BF16), preferred_element_type=F32) + run
    r1 = jnp.sum(jnp.where(sel1, before, 0.0), axis=1, keepdims=True)
    r2 = jnp.sum(jnp.where(sel2, before, 0.0), axis=1, keepdims=True)
    run = run + jnp.sum(chosen, axis=0, keepdims=True)
    run_ref[...] = run
    cnt_ref[...] = run

    meta = jnp.where(lane == 0, i1.astype(F32), 0.0)
    meta = jnp.where(lane == 1, i2.astype(F32), meta)
    meta = jnp.where(lane == 2, w1, meta)
    meta = jnp.where(lane == 3, w2, meta)
    meta = jnp.where(lane == 4, r1, meta)
    meta = jnp.where(lane == 5, r2, meta)
    meta_ref[...] = meta


def _router(x2, w_router):
    n = x2.shape[0]
    tm = TM_ROUTE
    wr = jnp.pad(w_router, ((0, 0), (0, LANES - N_EXPERTS)))
    r = lax.broadcasted_iota(jnp.int32, (tm, tm), 0)
    c = lax.broadcasted_iota(jnp.int32, (tm, tm), 1)
    tri = (c < r).astype(BF16)
    return pl.pallas_call(
        _router_kernel,
        grid=(n // tm,),
        in_specs=[pl.BlockSpec((tm, D_MODEL), lambda i: (i, 0)),
                  pl.BlockSpec((D_MODEL, LANES), lambda i: (0, 0)),
                  pl.BlockSpec((tm, tm), lambda i: (0, 0))],
        out_specs=[pl.BlockSpec((tm, LANES), lambda i: (i, 0)),
                   pl.BlockSpec((1, LANES), lambda i: (0, 0))],
        out_shape=[jax.ShapeDtypeStruct((n, LANES), F32),
                   jax.ShapeDtypeStruct((1, LANES), F32)],
        scratch_shapes=[pltpu.VMEM((1, LANES), F32)],
        compiler_params=_params("arbitrary"),
        name="router",
    )(x2, wr, tri)


def _row_copy(src_hbm, dst_ref, src_row, dst_row, sem):
    return pltpu.make_async_copy(src_hbm.at[pl.ds(src_row, 1)], dst_ref.at[pl.ds(dst_row, 1)], sem)


def _all_rows_copy(src_hbm, dst_ref, sem):
    return pltpu.make_async_copy(src_hbm.at[pl.ds(0, dst_ref.shape[0])], dst_ref, sem)


def _gather_kernel(tok_ref, x_hbm, o_ref, sem, *, g):
    def chunk(c, carry):
        for u in range(DMA_UNROLL):
            r = c * DMA_UNROLL + u
            _row_copy(x_hbm, o_ref, tok_ref[0, 0, r], r, sem).start(priority=u % 2)
        return carry

    lax.fori_loop(0, g // DMA_UNROLL, chunk, 0)
    _all_rows_copy(x_hbm, o_ref, sem).wait()


def _gather_rows(x2, tok):
    p = tok.shape[0]
    g = G_GATHER
    return pl.pallas_call(
        functools.partial(_gather_kernel, g=g),
        grid=(p // g,),
        in_specs=[pl.BlockSpec((1, 1, g), lambda i: (i, 0, 0), memory_space=pltpu.SMEM),
                  pl.BlockSpec(memory_space=pl.ANY)],
        out_specs=pl.BlockSpec((g, D_MODEL), lambda i: (i, 0)),
        out_shape=jax.ShapeDtypeStruct((p, D_MODEL), x2.dtype),
        scratch_shapes=[pltpu.SemaphoreType.DMA],
        compiler_params=_params("arbitrary"),
        name="moe_gather",
    )(tok.reshape(p // g, 1, g), x2)


def _moe_ffn_kernel(te_ref, nt_ref, x_ref, wg_ref, wu_ref, wd_ref, o_ref, acc_ref):
    i = pl.program_id(0)
    j = pl.program_id(1)

    @pl.when(i < nt_ref[0])
    def _():
        part = _swiglu_partial(x_ref[...].astype(BF16), wg_ref[0], wu_ref[0], wd_ref[0])

        @pl.when(j == 0)
        def _():
            acc_ref[...] = part

        @pl.when(j > 0)
        def _():
            acc_ref[...] += part

        @pl.when(j == pl.num_programs(1) - 1)
        def _():
            o_ref[...] = acc_ref[...]

    @pl.when(i >= nt_ref[0])
    def _():
        o_ref[...] = jnp.zeros(o_ref.shape, o_ref.dtype)


def _moe_ffn(xs, tile_expert, num_tiles, wg, wu, wd):
    p = xs.shape[0]
    tm, tf = TM_FFN, TF_FFN
    nj = D_FF // tf

    def row(i, j, te, nt):
        return (jnp.minimum(i, nt[0] - 1), 0)

    def col_j(i, j, nt):
        return jnp.where(i < nt[0], j, nj - 1)

    def w_up(i, j, te, nt):
        return (te[jnp.minimum(i, nt[0] - 1)], 0, col_j(i, j, nt))

    def w_down(i, j, te, nt):
        return (te[jnp.minimum(i, nt[0] - 1)], col_j(i, j, nt), 0)

    grid_spec = pltpu.PrefetchScalarGridSpec(
        num_scalar_prefetch=2,
        grid=(p // tm, nj),
        in_specs=[pl.BlockSpec((tm, D_MODEL), row),
                  pl.BlockSpec((1, D_MODEL, tf), w_up),
                  pl.BlockSpec((1, D_MODEL, tf), w_up),
                  pl.BlockSpec((1, tf, D_MODEL), w_down)],
        out_specs=pl.BlockSpec((tm, D_MODEL), lambda i, j, te, nt: (i, 0)),
        scratch_shapes=[pltpu.VMEM((tm, D_MODEL), F32)],
    )
    return pl.pallas_call(
        _moe_ffn_kernel,
        grid_spec=grid_spec,
        out_shape=jax.ShapeDtypeStruct((p, D_MODEL), F32),
        compiler_params=_params("arbitrary", "arbitrary"),
        name="moe_ffn",
    )(tile_expert, num_tiles, xs, wg, wu, wd)


def _combine_kernel(p1_ref, p2_ref, x_ref, w1_ref, w2_ref, y_hbm, lg_ref, lb_ref, o_ref,
                    buf1, buf2, sem, *, tc):
    def chunk(c, carry):
        for u in range(DMA_UNROLL):
            r = c * DMA_UNROLL + u
            _row_copy(y_hbm, buf1, p1_ref[0, 0, r], r, sem.at[0]).start(priority=0)
            _row_copy(y_hbm, buf2, p2_ref[0, 0, r], r, sem.at[1]).start(priority=1)
        return carry

    lax.fori_loop(0, tc // DMA_UNROLL, chunk, 0)
    _all_rows_copy(y_hbm, buf1, sem.at[0]).wait()
    _all_rows_copy(y_hbm, buf2, sem.at[1]).wait()
    f =w1_ref[...] * buf1[...] + w2_ref[...] * buf2[...]
    o_ref[...] = _layer_norm_rows(DN_ALPHA * x_ref[...] + f, lg_ref[...], lb_ref[...])


def _combine(x2, y, pos1, pos2, w1, w2, ln_g, ln_b):
    n = x2.shape[0]
    tc = TC_COMBINE
    idx = pl.BlockSpec((1, 1, tc), lambda i: (i, 0, 0), memory_space=pltpu.SMEM)
    return pl.pallas_call(
        functools.partial(_combine_kernel, tc=tc),
        grid=(n // tc,),
        in_specs=[idx, idx,
                  pl.BlockSpec((tc, D_MODEL), lambda i: (i, 0)),
                  pl.BlockSpec((tc, 1), lambda i: (i, 0)),
                  pl.BlockSpec((tc, 1), lambda i: (i, 0)),
                  pl.BlockSpec(memory_space=pl.ANY),
                  pl.BlockSpec((1, D_MODEL), lambda i: (0, 0)),
                  pl.BlockSpec((1, D_MODEL), lambda i: (0, 0))],
        out_specs=pl.BlockSpec((tc, D_MODEL), lambda i: (i, 0)),
        out_shape=jax.ShapeDtypeStruct((n, D_MODEL), F32),
        scratch_shapes=[pltpu.VMEM((tc, D_MODEL), F32), pltpu.VMEM((tc, D_MODEL), F32),
                        pltpu.SemaphoreType.DMA((2,))],
        compiler_params=_params("arbitrary"),
        name="moe_combine",
    )(pos1.reshape(n // tc, 1, tc), pos2.reshape(n // tc, 1, tc), x2, w1, w2, y,
      ln_g[None, :], ln_b[None, :])


def _moe_layer(x2, w_router, wg, wu, wd, ln_g, ln_b):
    n = x2.shape[0]
    tm = TM_FFN
    n_tiles = (2 * n) // tm + N_EXPERTS
    p = n_tiles * tm

    meta, counts = _router(x2, w_router)
    e1 = meta[:, 0].astype(jnp.int32)
    e2 = meta[:, 1].astype(jnp.int32)
    w1 = meta[:, 2:3]
    w2 = meta[:, 3:4]
    r1 = meta[:, 4].astype(jnp.int32)
    r2 = meta[:, 5].astype(jnp.int32)
    cnt = counts[0, :N_EXPERTS].astype(jnp.int32)

    tiles_per = (cnt + tm - 1) // tm
    tile_end = jnp.cumsum(tiles_per)
    start = (tile_end - tiles_per) * tm
    pos1 = start[e1] + r1
    pos2 = start[e2] + r2
    num_tiles = tile_end[-1:]
    tile_expert = jnp.minimum(
        jnp.sum((jnp.arange(n_tiles)[:, None] >= tile_end[None, :]).astype(jnp.int32), axis=1),
        N_EXPERTS - 1)
    token = jnp.arange(n, dtype=jnp.int32)
    tok = jnp.zeros((p,), jnp.int32).at[pos1].set(token).at[pos2].set(token)

    xs = _gather_rows(x2, tok)
    y = _moe_ffn(xs, tile_expert, num_tiles, wg, wu, wd)
    return _combine(x2, y, pos1, pos2, w1, w2, ln_g, ln_b)


def _prep_in_proj(w):
    sc = w[:, 0:1536]
    cf = w[:, 1536:2560]
    cq = w[:, 2560:2816]
    ckv = w[:, 2816:2944]
    kr = w[:, 2944:2976]
    sb = w[:, 2976:4512]
    gate = w[:, 4512:]
    half = MLA_ROPE // 2
    z_nope = jnp.zeros((D_MODEL, MLA_NOPE), F32)
    z_tail = jnp.zeros((D_MODEL, HEAD_PAD - MLA_NOPE - MLA_ROPE), F32)
    kr_pad = jnp.concatenate([z_nope, kr, z_tail], axis=1)
    kr_rot = jnp.concatenate([z_nope, -kr[:, half:], kr[:, :half], z_tail], axis=1)
    hw = SB_HEADS * SB_HEAD_DIM
    sbq = sb[:, :hw] * (-(SB_HEAD_DIM ** -0.5) * math.log2(math.e))
    mix = jnp.concatenate([sc, cf, cq, ckv, kr_pad, kr_rot, sbq, sb[:, hw:]], axis=1)
    return mix.astype(BF16), gate.astype(BF16)


def _prep_mla(w_uq, w_ukv):
    half = MLA_ROPE // 2
    wq = w_uq.reshape(MLA_Q_RANK, MLA_HEADS, MLA_NOPE + MLA_ROPE)
    nope, rope = wq[..., :MLA_NOPE], wq[..., MLA_NOPE:]
    tail = jnp.zeros((MLA_Q_RANK, MLA_HEADS, HEAD_PAD - MLA_NOPE - MLA_ROPE), F32)
    q_main = jnp.concatenate([nope, rope, tail], axis=-1)
    q_rot = jnp.concatenate([jnp.zeros_like(nope), -rope[..., half:], rope[..., :half], tail], axis=-1)
    wkv = w_ukv.reshape(MLA_KV_RANK, MLA_HEADS, MLA_NOPE + MLA_V)
    k_nope, v = wkv[..., :MLA_NOPE], wkv[..., MLA_NOPE:]
    k_main = jnp.concatenate(
        [k_nope, jnp.zeros((MLA_KV_RANK, MLA_HEADS, HEAD_PAD - MLA_NOPE), F32)], axis=-1)
    hw = MLA_HEADS * HEAD_PAD
    return (q_main.reshape(MLA_Q_RANK, hw).astype(BF16), q_rot.reshape(MLA_Q_RANK, hw).astype(BF16),
            k_main.reshape(MLA_KV_RANK, hw).astype(BF16),
            v.reshape(MLA_KV_RANK, MLA_HEADS * MLA_V).astype(BF16))


def kernel(x, positions, w_in, b_gate, sc_conv, cf_conv, cf_conv_bias, cf_ln_g, cf_ln_b,
           mla_q_norm, mla_w_uq, mla_kv_norm, mla_w_ukv, w_branch, w_out,
           ln_mix_g, ln_mix_b, ln_ffn_g, ln_ffn_b, ffn_w_gate, ffn_w_up, ffn_w_down,
           router_w, exp_w_gate, exp_w_up, exp_w_down):
    bsz, seq, d = x.shape
    n = bsz * seq
    cos, sin = _rope_tables(positions)
    x2 = x.reshape(n, d)
    for layer in range(DEPTH):
        w_mix, w_gate = _prep_in_proj(w_in[layer])
        wq, wqr, wk, wv = _prep_mla(mla_w_uq[layer], mla_w_ukv[layer])
        proj = _inproj(x2, w_mix).reshape(bsz, seq, N_MIX)
        br0, br1 = _conv_branches(proj, sc_conv[layer], cf_conv[layer], cf_conv_bias[layer],
                                  cf_ln_g[layer], cf_ln_b[layer])
        q, k, v = _mla_prep(proj, cos, sin, mla_q_norm[layer], mla_kv_norm[layer], wq, wqr, wk, wv)
        br2 = _mla_attention(q, k, v)
        br3 = _sb_attention(proj)
        branches = [b.reshape(n, BRANCH_WIDTH) for b in (br0, br1, br2, br3)]
        x2 = _merge(x2, branches, w_gate, b_gate[layer], w_branch[layer].astype(BF16),
                    w_out[layer].astype(BF16), ln_mix_g[layer], ln_mix_b[layer])
        i = layer // 2
        if layer % 2 == 0:
            x2 = _dense_ffn(x2, ffn_w_gate[i].astype(BF16), ffn_w_up[i].astype(BF16),
                            ffn_w_down[i].astype(BF16), ln_ffn_g[layer], ln_ffn_b[layer])
        else:
            x2 = _moe_layer(x2, router_w[i], exp_w_gate[i].astype(BF16), exp_w_up[i].astype(BF16),
                            exp_w_down[i].astype(BF16), ln_ffn_g[layer], ln_ffn_b[layer])
    return x2.reshape(bsz, seq, d)
```

```python
import functools
import math

import jax
import jax.numpy as jnp
from jax import lax
from jax.experimental import pallas as pl
from jax.experimental.pallas import tpu as pltpu

F32 = jnp.float32
BF16 = jnp.bfloat16

D_MODEL = 1024
DEPTH = 2
CHUNK = 64
BRANCH_WIDTH = 512
SC_CONV_LEN = 3
CF_CONV_LEN = 31
MLA_HEADS = 8
MLA_NOPE = 64
MLA_ROPE = 32
MLA_V = 64
MLA_Q_RANK = 256
MLA_KV_RANK = 128
ROPE_THETA = 10000.0
SB_HEADS = 8
SB_HEAD_DIM = 64
D_FF = 3584
N_EXPERTS = 8
DN_ALPHA = (2 * DEPTH) ** 0.25
LN_EPS = 1e-5
RMS_EPS = 1e-6

LANES = 128
SUBLANES = 8
HEAD_PAD = 128
CONV_HALO = 32
VMEM_LIMIT = 56 * 1024 * 1024

OFF_SCB, OFF_SCC, OFF_SCH = 0, 512, 1024
OFF_CFV, OFF_CFG = 1536, 2048
OFF_CQ = 2560
OFF_CKV = 2816
OFF_KR = 2944
OFF_KRR = 3072
OFF_SBQ, OFF_SBK, OFF_SBV = 3200, 3712, 4224
N_MIX = 4736

TM_PROJ = 512
TS_CONV = 512
TS_PREP = 512
T_ATT = 512
TM_MERGE = 512
TM_FFN = 512
TF_FFN = 1792
TM_ROUTE = 512
TT_DISPATCH = 512
TC_COMBINE = 256
DMA_UNROLL = 8

_NT = (((1,), (1,)), ((), ()))


def _params(*sem):
    return pltpu.CompilerParams(dimension_semantics=sem, vmem_limit_bytes=VMEM_LIMIT)


def _sigmoid(x):
    return 1.0 / (1.0 + jnp.exp(-x))


def _layer_norm_rows(v, g, b):
    mu = jnp.mean(v, axis=-1, keepdims=True)
    c = v - mu
    var = jnp.mean(c * c, axis=-1, keepdims=True)
    return c * lax.rsqrt(var + LN_EPS) * g + b


def _rope_kernel(pos_ref, freq_ref, cos_ref, sin_ref):
    ang = pos_ref[0] * freq_ref[...]
    cos_ref[0] = jnp.cos(ang)
    sin_ref[0] = jnp.sin(ang)


def _rope_tables(positions):
    bsz, seq = positions.shape
    half = MLA_ROPE // 2
    inv_freq = 1.0 / (ROPE_THETA ** (jnp.arange(half, dtype=F32) * (2.0 / MLA_ROPE)))
    freq = jnp.concatenate([jnp.zeros((MLA_NOPE,), F32), inv_freq, inv_freq,
                            jnp.zeros((HEAD_PAD - MLA_NOPE - MLA_ROPE,), F32)])[None, :]
    pos = positions.astype(F32)[..., None]
    ts = TS_PREP
    return pl.pallas_call(
        _rope_kernel,
        grid=(bsz, seq // ts),
        in_specs=[pl.BlockSpec((1, ts, 1), lambda b, i: (b, i, 0)),
                  pl.BlockSpec((1, HEAD_PAD), lambda b, i: (0, 0))],
        out_specs=[pl.BlockSpec((1, ts, HEAD_PAD), lambda b, i: (b, i, 0))] * 2,
        out_shape=[jax.ShapeDtypeStruct((bsz, seq, HEAD_PAD), F32)] * 2,
        compiler_params=_params("parallel", "parallel"),
        name="rope_tables",
    )(pos, freq)


def _inproj_kernel(x_ref, w_ref, o_ref):
    xb = x_ref[...].astype(BF16)
    for c0 in range(0, N_MIX, 512):
        c1 = min(c0 + 512, N_MIX)
        o_ref[:, c0:c1] = jnp.dot(xb, w_ref[:, c0:c1], preferred_element_type=F32).astype(BF16)


def _inproj(x2, w_mix):
    n = x2.shape[0]
    tm = TM_PROJ
    return pl.pallas_call(
        _inproj_kernel,
        grid=(n // tm,),
        in_specs=[pl.BlockSpec((tm, D_MODEL), lambda i: (i, 0)),
                  pl.BlockSpec((D_MODEL, N_MIX), lambda i: (0, 0))],
        out_specs=pl.BlockSpec((tm, N_MIX), lambda i: (i, 0)),
        out_shape=jax.ShapeDtypeStruct((n, N_MIX), BF16),
        compiler_params=_params("parallel"),
        name="in_proj",
    )(x2, w_mix)


def _causal_taps(w_ref, buf, n_taps, ts):
    groups = {}
    for l in range(n_taps):
        off = CONV_HALO - (n_taps - 1) + l
        groups.setdefault(off % SUBLANES, []).append((l, off - off % SUBLANES))
    out = None
    for shift, taps in sorted(groups.items()):
        part = None
        for l, base in taps:
            term = w_ref[l:l + 1, :] * buf[base:base + ts + SUBLANES, :]
            part = term if part is None else part + term
        part = part[shift:shift + ts, :]
        out = part if out is None else out + part
    return out


def _conv_kernel(b_ref, c_ref, h_ref, v_ref, g_ref, ch_ref, hh_ref, vh_ref, gh_ref,
                 wsc_ref, wcf_ref, bcf_ref, lng_ref, lnb_ref, o0_ref, o1_ref, buf, *, ts):
    has_history = pl.program_id(1) > 0
    halo = CONV_HALO
    buf[halo + ts:, :] = jnp.zeros((SUBLANES, buf.shape[1]), F32)

    hist = ch_ref[0].astype(F32) * hh_ref[0].astype(F32)
    buf[0:halo, :] = jnp.where(has_history, hist, 0.0)
    buf[halo:halo + ts, :] = c_ref[0].astype(F32) * h_ref[0].astype(F32)
    acc = _causal_taps(wsc_ref, buf, SC_CONV_LEN, ts)
    o0_ref[0] = (b_ref[0].astype(F32) * acc).astype(BF16)

    hist = vh_ref[0].astype(F32) * _sigmoid(gh_ref[0].astype(F32))
    buf[0:halo, :] = jnp.where(has_history, hist, 0.0)
    buf[halo:halo + ts, :] = v_ref[0].astype(F32) * _sigmoid(g_ref[0].astype(F32))
    acc = _causal_taps(wcf_ref, buf, CF_CONV_LEN, ts)
    y = _layer_norm_rows(acc + bcf_ref[...], lng_ref[...], lnb_ref[...])
    o1_ref[0] = (y * _sigmoid(y)).astype(BF16)


def _conv_branches(proj3, sc_conv, cf_conv, cf_bias, cf_ln_g, cf_ln_b):
    bsz, seq, _ = proj3.shape
    ts = TS_CONV
    w = BRANCH_WIDTH
    hb = ts // CONV_HALO

    def main(col):
        return pl.BlockSpec((1, ts, w), lambda b, i: (b, i, col))

    def halo(col):
        return pl.BlockSpec((1, CONV_HALO, w), lambda b, i: (b, jnp.maximum(i * hb - 1, 0), col))

    def vec(rows):
        return pl.BlockSpec((rows, w), lambda b, i: (0, 0))

    return pl.pallas_call(
        functools.partial(_conv_kernel, ts=ts),
        grid=(bsz, seq // ts),
        in_specs=[main(OFF_SCB // w), main(OFF_SCC // w), main(OFF_SCH // w),
                  main(OFF_CFV // w), main(OFF_CFG // w),
                  halo(OFF_SCC // w), halo(OFF_SCH // w), halo(OFF_CFV // w), halo(OFF_CFG // w),
                  vec(SC_CONV_LEN), vec(CF_CONV_LEN), vec(1), vec(1), vec(1)],
        out_specs=[pl.BlockSpec((1, ts, w), lambda b, i: (b, i, 0))] * 2,
        out_shape=[jax.ShapeDtypeStruct((bsz, seq, w), BF16)] * 2,
        scratch_shapes=[pltpu.VMEM((CONV_HALO + ts + SUBLANES, w), F32)],
        compiler_params=_params("parallel", "parallel"),
        name="conv_branches",
    )(proj3, proj3, proj3, proj3, proj3, proj3, proj3, proj3, proj3,
      sc_conv, cf_conv, cf_bias[None, :], cf_ln_g[None, :], cf_ln_b[None, :])


def _mla_prep_kernel(cq_ref, ckv_ref, kr_ref, krr_ref, cos_ref, sin_ref, qn_ref, kvn_ref,
                     wq_ref, wqr_ref, wk_ref, wv_ref, q_ref, k_ref, v_ref):
    cos = cos_ref[0]
    sin = sin_ref[0]
    cos8 = jnp.concatenate([cos] * MLA_HEADS, axis=1)
    sin8 = jnp.concatenate([sin] * MLA_HEADS, axis=1)
    scale = (MLA_NOPE + MLA_ROPE) ** -0.5 * math.log2(math.e)

    cq = cq_ref[0].astype(F32)
    nq = cq * lax.rsqrt(jnp.mean(cq * cq, axis=-1, keepdims=True) + RMS_EPS) * qn_ref[...]
    nq = nq.astype(BF16)
    q1 = jnp.dot(nq, wq_ref[...], preferred_element_type=F32)
    q2 = jnp.dot(nq, wqr_ref[...], preferred_element_type=F32)
    q_ref[0] = ((q1 * cos8 + q2 * sin8) * scale).astype(BF16)

    ckv = ckv_ref[0].astype(F32)
    nkv = ckv * lax.rsqrt(jnp.mean(ckv * ckv, axis=-1, keepdims=True) + RMS_EPS) * kvn_ref[...]
    nkv = nkv.astype(BF16)
    k_rope = kr_ref[0].astype(F32) * cos + krr_ref[0].astype(F32) * sin
    k_nope = jnp.dot(nkv, wk_ref[...], preferred_element_type=F32)
    k_ref[0] = (k_nope + jnp.concatenate([k_rope] * MLA_HEADS, axis=1)).astype(BF16)
    v_ref[0] = jnp.dot(nkv, wv_ref[...], preferred_element_type=F32).astype(BF16)


def _mla_prep(proj3, cos, sin, q_norm, kv_norm, wq, wqr, wk, wv):
    bsz, seq, _ = proj3.shape
    ts = TS_PREP
    hw = MLA_HEADS * HEAD_PAD
    vw = MLA_HEADS * MLA_V

    def const(shape):
        return pl.BlockSpec(shape, lambda b, i: (0, 0))

    return pl.pallas_call(
        _mla_prep_kernel,
        grid=(bsz, seq // ts),
        in_specs=[pl.BlockSpec((1, ts, MLA_Q_RANK), lambda b, i: (b, i, OFF_CQ // MLA_Q_RANK)),
                  pl.BlockSpec((1, ts, MLA_KV_RANK), lambda b, i: (b, i, OFF_CKV // MLA_KV_RANK)),
                  pl.BlockSpec((1, ts, HEAD_PAD), lambda b, i: (b, i, OFF_KR // HEAD_PAD)),
                  pl.BlockSpec((1, ts, HEAD_PAD), lambda b, i: (b, i, OFF_KRR // HEAD_PAD)),
                  pl.BlockSpec((1, ts, HEAD_PAD), lambda b, i: (b, i, 0)),
                  pl.BlockSpec((1, ts, HEAD_PAD), lambda b, i: (b, i, 0)),
                  const((1, MLA_Q_RANK)), const((1, MLA_KV_RANK)),
                  const((MLA_Q_RANK, hw)), const((MLA_Q_RANK, hw)),
                  const((MLA_KV_RANK, hw)), const((MLA_KV_RANK, vw))],
        out_specs=[pl.BlockSpec((1, ts, hw), lambda b, i: (b, i, 0)),
                   pl.BlockSpec((1, ts, hw), lambda b, i: (b, i, 0)),
                   pl.BlockSpec((1, ts, vw), lambda b, i: (b, i, 0))],
        out_shape=[jax.ShapeDtypeStruct((bsz, seq, hw), BF16),
                   jax.ShapeDtypeStruct((bsz, seq, hw), BF16),
                   jax.ShapeDtypeStruct((bsz, seq, vw), BF16)],
        compiler_params=_params("parallel", "parallel"),
        name="mla_prep",
    )(proj3, proj3, proj3, proj3, cos, sin, q_norm[None, :], kv_norm[None, :], wq, wqr, wk, wv)


def _mla_attn_kernel(q_ref, k_ref, v_ref, o_ref, m_ref, acc_ref, *, t):
    qi = pl.program_id(2)
    row = lax.broadcasted_iota(jnp.int32, (t, t), 0)
    col = lax.broadcasted_iota(jnp.int32, (t, t), 1)
    allowed = (col // CHUNK) <= (row // CHUNK)
    ones = jnp.ones((t, LANES), BF16)
    m_ref[...] = jnp.full(m_ref.shape, -jnp.inf, F32)
    acc_ref[...] = jnp.zeros(acc_ref.shape, F32)

    def block(kstart, masked):
        v_ext = jnp.concatenate([v_ref[0, pl.ds(kstart, t), :], ones], axis=1)
        for hh in range(2):
            lo = hh * HEAD_PAD
            qh = q_ref[0, :, lo:lo + HEAD_PAD]
            kblk = k_ref[0, pl.ds(kstart, t), lo:lo + HEAD_PAD]
            s = lax.dot_general(qh, kblk, _NT, preferred_element_type=F32)
            if masked:
                s = jnp.where(allowed, s, -jnp.inf)
            m_prev = m_ref[hh]
            m_new = jnp.maximum(m_prev, jnp.max(s, axis=1, keepdims=True))
            alpha = jnp.exp2(m_prev - m_new)
            p = jnp.exp2(s - jnp.concatenate([m_new] * (t // LANES), axis=1))
            acc_ref[hh] = (jnp.concatenate([alpha, alpha], axis=1) * acc_ref[hh]
                           + jnp.dot(p.astype(BF16), v_ext, preferred_element_type=F32))
            m_ref[hh] = m_new

    def pair(i, carry):
        block(pl.multiple_of(2 * i * t, t), False)
        block(pl.multiple_of((2 * i + 1) * t, t), False)
        return carry

    lax.fori_loop(0, lax.shift_right_logical(qi, 1), pair, 0)

    @pl.when((qi & 1) == 1)
    def _():
        block(pl.multiple_of((qi - 1) * t, t), False)

    block(pl.multiple_of(qi * t, t), True)
    pv = 2 * MLA_V
    lane = lax.broadcasted_iota(jnp.int32, (t, pv), 1)
    out0 = acc_ref[0, :, :pv] / acc_ref[0, :, pv:]
    out1 = acc_ref[1, :, :pv] / acc_ref[1, :, pv:]
    o_ref[0] = jnp.where(lane < MLA_V, out0, out1).astype(BF16)


def _mla_attention(q, k, v):
    bsz, seq, _ = q.shape
    t = T_ATT
    pairs = MLA_HEADS // 2
    pv = 2 * MLA_V
    return pl.pallas_call(
        functools.partial(_mla_attn_kernel, t=t),
        grid=(bsz, pairs, seq // t),
        in_specs=[pl.BlockSpec((1, t, 2 * HEAD_PAD), lambda b, h, i: (b, i, h)),
                  pl.BlockSpec((1, seq, 2 * HEAD_PAD), lambda b, h, i: (b, 0, h)),
                  pl.BlockSpec((1, seq, pv), lambda b, h, i: (b, 0, h))],
        out_specs=pl.BlockSpec((1, t, pv), lambda b, h, i: (b, i, h)),
        out_shape=jax.ShapeDtypeStruct((bsz, seq, MLA_HEADS * MLA_V), BF16),
        scratch_shapes=[pltpu.VMEM((2, t, LANES), F32), pltpu.VMEM((2, t, pv + LANES), F32)],
        compiler_params=_params("parallel", "parallel", "arbitrary"),
        name="mla_attention",
    )(q, k, v)


def _sb_attn_kernel(q_ref, k_ref, v_ref, u_ref, o_ref, acc_ref, carry_ref, *, t):
    qi = pl.program_id(2)
    row = lax.broadcasted_iota(jnp.int32, (t, t), 0)
    col = lax.broadcasted_iota(jnp.int32, (t, t), 1)
    before = col < row
    lane = lax.broadcasted_iota(jnp.int32, (t, 2 * SB_HEAD_DIM), 1)
    half = t // 2
    reps = half // LANES
    acc_ref[...] = jnp.zeros(acc_ref.shape, F32)
    carry_ref[...] = jnp.zeros(carry_ref.shape, F32)

    def block(kstart, masked):
        kblk = k_ref[0, pl.ds(kstart, t), :]
        vblk = v_ref[0, pl.ds(kstart, t), :]
        q = q_ref[0]
        for hh in range(2):
            in_head = (lane >= hh * SB_HEAD_DIM) & (lane < (hh + 1) * SB_HEAD_DIM)
            qm = jnp.where(in_head, q, jnp.zeros_like(q))
            nz = lax.dot_general(qm, kblk, _NT, preferred_element_type=F32)
            log_stay = jnp.minimum(nz, 0.0) - jnp.log2(1.0 + jnp.exp2(-jnp.abs(nz)))
            if masked:
                log_stay = jnp.where(before, log_stay, 0.0)
            stay_lo, stay_hi = log_stay[:, :half], log_stay[:, half:]
            sums_hi = jnp.dot(stay_hi.astype(BF16), u_ref[...], preferred_element_type=F32)
            sums_lo = jnp.dot(stay_lo.astype(BF16), u_ref[...], preferred_element_type=F32)
            carry_hi = carry_ref[hh]
            carry_lo = carry_hi + jnp.sum(stay_hi, axis=1, keepdims=True)
            arg_hi = sums_hi + (jnp.concatenate([carry_hi] * reps, axis=1) - nz[:, half:])
            arg_lo = sums_lo + (jnp.concatenate([carry_lo] * reps, axis=1) - nz[:, :half])
            a = jnp.exp2(jnp.concatenate([arg_lo, arg_hi], axis=1))
            if masked:
                a = jnp.where(before, a, 0.0)
            acc_ref[hh] += jnp.dot(a.astype(BF16), vblk, preferred_element_type=F32)
            carry_ref[hh] = carry_lo + jnp.sum(stay_lo, axis=1, keepdims=True)

    block(pl.multiple_of(qi * t, t), True)

    def pair(i, c):
        block(pl.multiple_of((qi - 1 - 2 * i) * t, t), False)
        block(pl.multiple_of((qi - 2 - 2 * i) * t, t), False)
        return c

    lax.fori_loop(0, lax.shift_right_logical(qi, 1), pair, 0)

    @pl.when((qi & 1) == 1)
    def _():
        block(0, False)

    o_ref[0] = jnp.where(lane < SB_HEAD_DIM, acc_ref[0], acc_ref[1]).astype(BF16)


def _sb_attention(proj3):
    bsz, seq, _ = proj3.shape
    t = T_ATT
    half = t // 2
    pairs = SB_HEADS // 2
    pw = 2 * SB_HEAD_DIM
    r = lax.broadcasted_iota(jnp.int32, (half, half), 0)
    c = lax.broadcasted_iota(jnp.int32, (half, half), 1)
    suffix = (r >= c).astype(BF16)
    return pl.pallas_call(
        functools.partial(_sb_attn_kernel, t=t),
        grid=(bsz, pairs, seq // t),
        in_specs=[pl.BlockSpec((1, t, pw), lambda b, h, i: (b, i, OFF_SBQ // pw + h)),
                  pl.BlockSpec((1, seq, pw), lambda b, h, i: (b, 0, OFF_SBK // pw + h)),
                  pl.BlockSpec((1, seq, pw), lambda b, h, i: (b, 0, OFF_SBV // pw + h)),
                  pl.BlockSpec((half, half), lambda b, h, i: (0, 0))],
        out_specs=pl.BlockSpec((1, t, pw), lambda b, h, i: (b, i, h)),
        out_shape=jax.ShapeDtypeStruct((bsz, seq, SB_HEADS * SB_HEAD_DIM), BF16),
        scratch_shapes=[pltpu.VMEM((2, t, pw), F32), pltpu.VMEM((2, t, LANES), F32)],
        compiler_params=_params("parallel", "parallel", "arbitrary"),
        name="sb_attention",
    )(proj3, proj3, proj3, suffix)


def _merge_kernel(x_ref, b0_ref, b1_ref, b2_ref, b3_ref, wg_ref, bg_ref, wb_ref, wo_ref,
                  lg_ref, lb_ref, o_ref):
    x = x_ref[...]
    xb = x.astype(BF16)
    merged = None
    for n, br in enumerate((b0_ref, b1_ref, b2_ref, b3_ref)):
        c0 = n * D_MODEL
        pre = jnp.dot(xb, wg_ref[:, c0:c0 + D_MODEL], preferred_element_type=F32)
        gate = _sigmoid(pre + bg_ref[:, c0:c0 + D_MODEL])
        term = gate * jnp.dot(br[...], wb_ref[n], preferred_element_type=F32)
        merged = term if merged is None else merged + term
    y = jnp.dot(merged.astype(BF16), wo_ref[...], preferred_element_type=F32)
    o_ref[...] = _layer_norm_rows(DN_ALPHA * x + y, lg_ref[...], lb_ref[...])


def _merge(x2, branches, w_gate, b_gate, w_branch, w_out, ln_g, ln_b):
    n = x2.shape[0]
    tm = TM_MERGE
    nb = len(branches)

    def const2(shape):
        return pl.BlockSpec(shape, lambda i: (0, 0))

    return pl.pallas_call(
        _merge_kernel,
        grid=(n // tm,),
        in_specs=[pl.BlockSpec((tm, D_MODEL), lambda i: (i, 0))]
                 + [pl.BlockSpec((tm, BRANCH_WIDTH), lambda i: (i, 0))] * nb
                 + [const2((D_MODEL, nb * D_MODEL)), const2((1, nb * D_MODEL)),
                    pl.BlockSpec((nb, BRANCH_WIDTH, D_MODEL), lambda i: (0, 0, 0)),
                    const2((D_MODEL, D_MODEL)), const2((1, D_MODEL)), const2((1, D_MODEL))],
        out_specs=pl.BlockSpec((tm, D_MODEL), lambda i: (i, 0)),
        out_shape=jax.ShapeDtypeStruct((n, D_MODEL), F32),
        compiler_params=_params("parallel"),
        name="merge_out",
    )(x2, *branches, w_gate, b_gate[None, :], w_branch, w_out, ln_g[None, :], ln_b[None, :])


def _swiglu_partial(xb, wg, wu, wd):
    g = jnp.dot(xb, wg, preferred_element_type=F32)
    u = jnp.dot(xb, wu, preferred_element_type=F32)
    h = (g * _sigmoid(g) * u).astype(BF16)
    return jnp.dot(h, wd, preferred_element_type=F32)


def _dense_ffn_kernel(x_ref, wg_ref, wu_ref, wd_ref, lg_ref, lb_ref, o_ref, acc_ref):
    j = pl.program_id(1)
    part = _swiglu_partial(x_ref[...].astype(BF16), wg_ref[...], wu_ref[...], wd_ref[...])

    @pl.when(j == 0)
    def _():
        acc_ref[...] = part

    @pl.when(j > 0)
    def _():
        acc_ref[...] += part

    @pl.when(j == pl.num_programs(1) - 1)
    def _():
        o_ref[...] = _layer_norm_rows(DN_ALPHA * x_ref[...] + acc_ref[...], lg_ref[...], lb_ref[...])


def _dense_ffn(x2, wg, wu, wd, ln_g, ln_b):
    n = x2.shape[0]
    tm, tf = TM_FFN, TF_FFN
    return pl.pallas_call(
        _dense_ffn_kernel,
        grid=(n // tm, D_FF // tf),
        in_specs=[pl.BlockSpec((tm, D_MODEL), lambda i, j: (i, 0)),
                  pl.BlockSpec((D_MODEL, tf), lambda i, j: (0, j)),
                  pl.BlockSpec((D_MODEL, tf), lambda i, j: (0, j)),
                  pl.BlockSpec((tf, D_MODEL), lambda i, j: (j, 0)),
                  pl.BlockSpec((1, D_MODEL), lambda i, j: (0, 0)),
                  pl.BlockSpec((1, D_MODEL), lambda i, j: (0, 0))],
        out_specs=pl.BlockSpec((tm, D_MODEL), lambda i, j: (i, 0)),
        out_shape=jax.ShapeDtypeStruct((n, D_MODEL), F32),
        scratch_shapes=[pltpu.VMEM((tm, D_MODEL), F32)],
        compiler_params=_params("parallel", "arbitrary"),
        name="dense_ffn",
    )(x2, wg, wu, wd, ln_g[None, :], ln_b[None, :])


def _router_kernel(x_ref, wr_ref, tri_ref, meta_ref, cnt_ref, run_ref):
    @pl.when(pl.program_id(0) == 0)
    def _():
        run_ref[...] = jnp.zeros(run_ref.shape, F32)

    x = x_ref[...]
    w = wr_ref[...]
    x_hi = x.astype(BF16)
    x_lo = (x - x_hi.astype(F32)).astype(BF16)
    w_hi = w.astype(BF16)
    w_lo = (w - w_hi.astype(F32)).astype(BF16)
    logits = (jnp.dot(x_hi, w_hi, preferred_element_type=F32)
              + jnp.dot(x_lo, w_hi, preferred_element_type=F32)
              + jnp.dot(x_hi, w_lo, preferred_element_type=F32))
    lane = lax.broadcasted_iota(jnp.int32, logits.shape, 1)
    logits = jnp.where(lane < N_EXPERTS, logits, -jnp.inf)
    v1 = jnp.max(logits, axis=1, keepdims=True)
    i1 = jnp.min(jnp.where(logits == v1, lane, LANES), axis=1, keepdims=True)
    rest = jnp.where(lane == i1, -jnp.inf, logits)
    v2 = jnp.max(rest, axis=1, keepdims=True)
    i2 = jnp.min(jnp.where(rest == v2, lane, LANES), axis=1, keepdims=True)
    ex = jnp.exp(v2 - v1)
    den = 1.0 + ex
    w1 = 1.0 / den
    w2 = ex / den

    sel1 = lane == i1
    sel2 = lane == i2
    chosen = jnp.where(sel1 | sel2, 1.0, 0.0)
    run = run_ref[...]
    before = jnp.dot(tri_ref[...], chosen.astype(BF16), preferred_element_type=F32) + run
    r1 = jnp.sum(jnp.where(sel1, before, 0.0), axis=1, keepdims=True)
    r2 = jnp.sum(jnp.where(sel2, before, 0.0), axis=1, keepdims=True)
    run = run + jnp.sum(chosen, axis=0, keepdims=True)
    run_ref[...] = run
    cnt_ref[...] = run

    meta = jnp.where(lane == 0, i1.astype(F32), 0.0)
    meta = jnp.where(lane == 1, i2.astype(F32), meta)
    meta = jnp.where(lane == 2, w1, meta)
    meta = jnp.where(lane == 3, w2, meta)
    meta = jnp.where(lane == 4, r1, meta)
    meta = jnp.where(lane == 5, r2, meta)
    meta_ref[...] = meta


def _router(x2, w_router):
    n = x2.shape[0]
    tm = TM_ROUTE
    wr = jnp.pad(w_router, ((0, 0), (0, LANES - N_EXPERTS)))
    r = lax.broadcasted_iota(jnp.int32, (tm, tm), 0)
    c = lax.broadcasted_iota(jnp.int32, (tm, tm), 1)
    tri = (c < r).astype(BF16)
    return pl.pallas_call(
        _router_kernel,
        grid=(n // tm,),
        in_specs=[pl.BlockSpec((tm, D_MODEL), lambda i: (i, 0)),
                  pl.BlockSpec((D_MODEL, LANES), lambda i: (0, 0)),
                  pl.BlockSpec((tm, tm), lambda i: (0, 0))],
        out_specs=[pl.BlockSpec((tm, LANES), lambda i: (i, 0)),
                   pl.BlockSpec((1, LANES), lambda i: (0, 0))],
        out_shape=[jax.ShapeDtypeStruct((n, LANES), F32),
                   jax.ShapeDtypeStruct((1, LANES), F32)],
        scratch_shapes=[pltpu.VMEM((1, LANES), F32)],
        compiler_params=_params("arbitrary"),
        name="router",
    )(x2, wr, tri)


def _row_copy(src_hbm, dst_ref, src_row, dst_row, sem):
    return pltpu.make_async_copy(src_hbm.at[pl.ds(src_row, 1)], dst_ref.at[pl.ds(dst_row, 1)], sem)


def _all_rows_copy(src_hbm, dst_ref, sem):
    return pltpu.make_async_copy(src_hbm.at[pl.ds(0, dst_ref.shape[0])], dst_ref, sem)


def _dispatch_kernel(p1_ref, p2_ref, x_ref, init_hbm, o_hbm, sem, *, tt):
    del init_hbm

    def chunk(c, carry):
        for u in range(DMA_UNROLL):
            r = c * DMA_UNROLL + u
            src = x_ref.at[pl.ds(r, 1)]
            pltpu.make_async_copy(src, o_hbm.at[pl.ds(p1_ref[0, 0, r], 1)], sem.at[0]).start(priority=0)
            pltpu.make_async_copy(src, o_hbm.at[pl.ds(p2_ref[0, 0, r], 1)], sem.at[1]).start(priority=1)
        return carry

    lax.fori_loop(0, tt // DMA_UNROLL, chunk, 0)
    pltpu.make_async_copy(x_ref, o_hbm.at[pl.ds(0, tt)], sem.at[0]).wait()
    pltpu.make_async_copy(x_ref, o_hbm.at[pl.ds(0, tt)], sem.at[1]).wait()


def _dispatch_rows(x2, pos1, pos2, p):
    n = x2.shape[0]
    tt = TT_DISPATCH
    idx = pl.BlockSpec((1, 1, tt), lambda i: (i, 0, 0), memory_space=pltpu.SMEM)
    return pl.pallas_call(
        functools.partial(_dispatch_kernel, tt=tt),
        grid=(n // tt,),
        in_specs=[idx, idx,
                  pl.BlockSpec((tt, D_MODEL), lambda i: (i, 0)),
                  pl.BlockSpec(memory_space=pl.ANY)],
        out_specs=pl.BlockSpec(memory_space=pl.ANY),
        out_shape=jax.ShapeDtypeStruct((p, D_MODEL), x2.dtype),
        input_output_aliases={3: 0},
        scratch_shapes=[pltpu.SemaphoreType.DMA((2,))],
        compiler_params=_params("arbitrary"),
        name="moe_dispatch",
    )(pos1.reshape(n // tt, 1, tt), pos2.reshape(n // tt, 1, tt), x2, jnp.zeros((p, D_MODEL), x2.dtype))


def _moe_ffn_kernel(te_ref, nt_ref, x_ref, wg_ref, wu_ref, wd_ref, o_ref, acc_ref):
    i = pl.program_id(0)
    j = pl.program_id(1)

    @pl.when(i < nt_ref[0])
    def _():
        part = _swiglu_partial(x_ref[...].astype(BF16), wg_ref[0], wu_ref[0], wd_ref[0])

        @pl.when(j == 0)
        def _():
            acc_ref[...] = part

        @pl.when(j > 0)
        def _():
            acc_ref[...] += part

        @pl.when(j == pl.num_programs(1) - 1)
        def _():
            o_ref[...] = acc_ref[...]

    @pl.when(i >= nt_ref[0])
    def _():
        o_ref[...] = jnp.zeros(o_ref.shape, o_ref.dtype)


def _moe_ffn(xs, tile_expert, num_tiles, wg, wu, wd):
    p = xs.shape[0]
    tm, tf = TM_FFN, TF_FFN
    nj = D_FF // tf

    def row(i, j, te, nt):
        return (jnp.minimum(i, nt[0] - 1), 0)

    def col_j(i, j, nt):
        return jnp.where(i < nt[0], j, nj - 1)

    def w_up(i, j, te, nt):
        return (te[jnp.minimum(i, nt[0] - 1)], 0, col_j(i, j, nt))

    def w_down(i, j, te, nt):
        return (te[jnp.minimum(i, nt[0] - 1)], col_j(i, j, nt), 0)

    grid_spec = pltpu.PrefetchScalarGridSpec(
        num_scalar_prefetch=2,
        grid=(p // tm, nj),
        in_specs=[pl.BlockSpec((tm, D_MODEL), row),
                  pl.BlockSpec((1, D_MODEL, tf), w_up),
                  pl.BlockSpec((1, D_MODEL, tf), w_up),
                  pl.BlockSpec((1, tf, D_MODEL), w_down)],
        out_specs=pl.BlockSpec((tm, D_MODEL), lambda i, j, te, nt: (i, 0)),
        scratch_shapes=[pltpu.VMEM((tm, D_MODEL), F32)],
    )
    return pl.pallas_call(
        _moe_ffn_kernel,
        grid_spec=grid_spec,
        out_shape=jax.ShapeDtypeStruct((p, D_MODEL), F32),
        compiler_params=_params("arbitrary", "arbitrary"),
        name="moe_ffn",
    )(tile_expert, num_tiles, xs, wg, wu, wd)


def _combine_kernel(p1_ref, p2_ref, x_ref, w1_ref, w2_ref, y_hbm, lg_ref, lb_ref, o_ref,
                    buf1, buf2, sem, *, tc):
    def chunk(c, carry):
        for u in range(DMA_UNROLL):
            r = c * DMA_UNROLL + u
            _row_copy(y_hbm, buf1, p1_ref[0, 0, r], r, sem.at[0]).start(priority=0)
            _row_copy(y_hbm, buf2, p2_ref[0, 0, r], r, sem.at[1]).start(priority=1)
        return carry

    lax.fori_loop(0, tc // DMA_UNROLL, chunk, 0)
    _all_rows_copy(y_hbm, buf1, sem.at[0]).wait()
    _all_rows_copy(y_hbm, buf2, sem.at[1]).wait()
    f =w1_ref[...] * buf1[...] + w2_ref[...] * buf2[...]
    o_ref[...] = _layer_norm_rows(DN_ALPHA * x_ref[...] + f, lg_ref[...], lb_ref[...])


def _combine(x2, y, pos1, pos2, w1, w2, ln_g, ln_b):
    n = x2.shape[0]
    tc = TC_COMBINE
    idx = pl.BlockSpec((1, 1, tc), lambda i: (i, 0, 0), memory_space=pltpu.SMEM)
    return pl.pallas_call(
        functools.partial(_combine_kernel, tc=tc),
        grid=(n // tc,),
        in_specs=[idx, idx,
                  pl.BlockSpec((tc, D_MODEL), lambda i: (i, 0)),
                  pl.BlockSpec((tc, 1), lambda i: (i, 0)),
                  pl.BlockSpec((tc, 1), lambda i: (i, 0)),
                  pl.BlockSpec(memory_space=pl.ANY),
                  pl.BlockSpec((1, D_MODEL), lambda i: (0, 0)),
                  pl.BlockSpec((1, D_MODEL), lambda i: (0, 0))],
        out_specs=pl.BlockSpec((tc, D_MODEL), lambda i: (i, 0)),
        out_shape=jax.ShapeDtypeStruct((n, D_MODEL), F32),
        scratch_shapes=[pltpu.VMEM((tc, D_MODEL), F32), pltpu.VMEM((tc, D_MODEL), F32),
                        pltpu.SemaphoreType.DMA((2,))],
        compiler_params=_params("arbitrary"),
        name="moe_combine",
    )(pos1.reshape(n // tc, 1, tc), pos2.reshape(n // tc, 1, tc), x2, w1, w2, y,
      ln_g[None, :], ln_b[None, :])


def _moe_layer(x2, w_router, wg, wu, wd, ln_g, ln_b):
    n = x2.shape[0]
    tm = TM_FFN
    n_tiles = (2 * n) // tm + N_EXPERTS
    p = n_tiles * tm

    meta, counts = _router(x2, w_router)
    e1 = meta[:, 0].astype(jnp.int32)
    e2 = meta[:, 1].astype(jnp.int32)
    w1 = meta[:, 2:3]
    w2 = meta[:, 3:4]
    r1 = meta[:, 4].astype(jnp.int32)
    r2 = meta[:, 5].astype(jnp.int32)
    cnt = counts[0, :N_EXPERTS].astype(jnp.int32)

    tiles_per = (cnt + tm - 1) // tm
    tile_end = jnp.cumsum(tiles_per)
    start = (tile_end - tiles_per) * tm
    pos1 = start[e1] + r1
    pos2 = start[e2] + r2
    num_tiles = tile_end[-1:]
    tile_expert = jnp.minimum(
        jnp.sum((jnp.arange(n_tiles)[:, None] >= tile_end[None, :]).astype(jnp.int32), axis=1),
        N_EXPERTS - 1)
    xs = _dispatch_rows(x2, pos1, pos2, p)
    y = _moe_ffn(xs, tile_expert, num_tiles, wg, wu, wd)
    return _combine(x2, y, pos1, pos2, w1, w2, ln_g, ln_b)


def _prep_in_proj(w):
    sc = w[:, 0:1536]
    cf = w[:, 1536:2560]
    cq = w[:, 2560:2816]
    ckv = w[:, 2816:2944]
    kr = w[:, 2944:2976]
    sb = w[:, 2976:4512]
    gate = w[:, 4512:]
    half = MLA_ROPE // 2
    z_nope = jnp.zeros((D_MODEL, MLA_NOPE), F32)
    z_tail = jnp.zeros((D_MODEL, HEAD_PAD - MLA_NOPE - MLA_ROPE), F32)
    kr_pad = jnp.concatenate([z_nope, kr, z_tail], axis=1)
    kr_rot = jnp.concatenate([z_nope, -kr[:, half:], kr[:, :half], z_tail], axis=1)
    hw = SB_HEADS * SB_HEAD_DIM
    sbq = sb[:, :hw] * (-(SB_HEAD_DIM ** -0.5) * math.log2(math.e))
    mix = jnp.concatenate([sc, cf, cq, ckv, kr_pad, kr_rot, sbq, sb[:, hw:]], axis=1)
    return mix.astype(BF16), gate.astype(BF16)


def _prep_mla(w_uq, w_ukv):
    half = MLA_ROPE // 2
    wq = w_uq.reshape(MLA_Q_RANK, MLA_HEADS, MLA_NOPE + MLA_ROPE)
    nope, rope = wq[..., :MLA_NOPE], wq[..., MLA_NOPE:]
    tail = jnp.zeros((MLA_Q_RANK, MLA_HEADS, HEAD_PAD - MLA_NOPE - MLA_ROPE), F32)
    q_main = jnp.concatenate([nope, rope, tail], axis=-1)
    q_rot = jnp.concatenate([jnp.zeros_like(nope), -rope[..., half:], rope[..., :half], tail], axis=-1)
    wkv = w_ukv.reshape(MLA_KV_RANK, MLA_HEADS, MLA_NOPE + MLA_V)
    k_nope, v = wkv[..., :MLA_NOPE], wkv[..., MLA_NOPE:]
    k_main = jnp.concatenate(
        [k_nope, jnp.zeros((MLA_KV_RANK, MLA_HEADS, HEAD_PAD - MLA_NOPE), F32)], axis=-1)
    hw = MLA_HEADS * HEAD_PAD
    return (q_main.reshape(MLA_Q_RANK, hw).astype(BF16), q_rot.reshape(MLA_Q_RANK, hw).astype(BF16),
            k_main.reshape(MLA_KV_RANK, hw).astype(BF16),
            v.reshape(MLA_KV_RANK, MLA_HEADS * MLA_V).astype(BF16))


def kernel(x, positions, w_in, b_gate, sc_conv, cf_conv, cf_conv_bias, cf_ln_g, cf_ln_b,
           mla_q_norm, mla_w_uq, mla_kv_norm, mla_w_ukv, w_branch, w_out,
           ln_mix_g, ln_mix_b, ln_ffn_g, ln_ffn_b, ffn_w_gate, ffn_w_up, ffn_w_down,
           router_w, exp_w_gate, exp_w_up, exp_w_down):
    bsz, seq, d = x.shape
    n = bsz * seq
    cos, sin = _rope_tables(positions)
    x2 = x.reshape(n, d)
    for layer in range(DEPTH):
        w_mix, w_gate = _prep_in_proj(w_in[layer])
        wq, wqr, wk, wv = _prep_mla(mla_w_uq[layer], mla_w_ukv[layer])
        proj = _inproj(x2, w_mix).reshape(bsz, seq, N_MIX)
        br0, br1 = _conv_branches(proj, sc_conv[layer], cf_conv[layer], cf_conv_bias[layer],
                                  cf_ln_g[layer], cf_ln_b[layer])
        q, k, v = _mla_prep(proj, cos, sin, mla_q_norm[layer], mla_kv_norm[layer], wq, wqr, wk, wv)
        br2 = _mla_attention(q, k, v)
        br3 = _sb_attention(proj)
        branches = [b.reshape(n, BRANCH_WIDTH) for b in (br0, br1, br2, br3)]
        x2 = _merge(x2, branches, w_gate, b_gate[layer], w_branch[layer].astype(BF16),
                    w_out[layer].astype(BF16), ln_mix_g[layer], ln_mix_b[layer])
        i = layer // 2
        if layer % 2 == 0:
            x2 = _dense_ffn(x2, ffn_w_gate[i].astype(BF16), ffn_w_up[i].astype(BF16),
                            ffn_w_down[i].astype(BF16), ln_ffn_g[layer], ln_ffn_b[layer])
        else:
            x2 = _moe_layer(x2, router_w[i], exp_w_gate[i].astype(BF16), exp_w_up[i].astype(BF16),
                            exp_w_down[i].astype(BF16), ln_ffn_g[layer], ln_ffn_b[layer])
    return x2.reshape(bsz, seq, d)
```

```python
import functools
import math

import jax
import jax.numpy as jnp
from jax import lax
from jax.experimental import pallas as pl
from jax.experimental.pallas import tpu as pltpu

F32 = jnp.float32
BF16 = jnp.bfloat16

D_MODEL = 1024
DEPTH = 2
CHUNK = 64
BRANCH_WIDTH = 512
SC_CONV_LEN = 3
CF_CONV_LEN = 31
MLA_HEADS = 8
MLA_NOPE = 64
MLA_ROPE = 32
MLA_V = 64
MLA_Q_RANK = 256
MLA_KV_RANK = 128
ROPE_THETA = 10000.0
SB_HEADS = 8
SB_HEAD_DIM = 64
D_FF = 3584
N_EXPERTS = 8
DN_ALPHA = (2 * DEPTH) ** 0.25
LN_EPS = 1e-5
RMS_EPS = 1e-6
SB_DEAD_LOG2 = -150.0

LANES = 128
SUBLANES = 8
HEAD_PAD = 128
CONV_HALO = 32
VMEM_LIMIT = 56 * 1024 * 1024

OFF_SCB, OFF_SCC, OFF_SCH = 0, 512, 1024
OFF_CFV, OFF_CFG = 1536, 2048
OFF_SBQ, OFF_SBK, OFF_SBV = 2560, 3072, 3584
OFF_CQ = 4096
OFF_CKV = 4352
OFF_KR = 4480
OFF_KRR = 4608
N_MIX = 4736

TM_PROJ = 512
TS_CONV = 512
TS_PREP = 512
T_ATT = 512
T_SB = 256
SB_HEADS_PER_STEP = 8
TM_MERGE = 512
TM_FFN = 512
TF_FFN = 1792
FFN_ROW_CHUNKS = 2
MERGE_ROW_CHUNKS = 2
TM_ROUTE = 512
TT_DISPATCH = 512
TC_COMBINE = 256
DMA_UNROLL = 8

_NT = (((1,), (1,)), ((), ()))


def _params(*sem):
    return pltpu.CompilerParams(dimension_semantics=sem, vmem_limit_bytes=VMEM_LIMIT)


def _sigmoid(x):
    return 1.0 / (1.0 + jnp.exp(-x))


def _layer_norm_rows(v, g, b):
    mu = jnp.mean(v, axis=-1, keepdims=True)
    c = v - mu
    var = jnp.mean(c * c, axis=-1, keepdims=True)
    return c * lax.rsqrt(var + LN_EPS) * g + b


def _rope_kernel(pos_ref, freq_ref, cos_ref, sin_ref):
    ang = pos_ref[0] * freq_ref[...]
    cos_ref[0] = jnp.cos(ang)
    sin_ref[0] = jnp.sin(ang)


def _rope_tables(positions):
    bsz, seq = positions.shape
    half = MLA_ROPE // 2
    inv_freq = 1.0 / (ROPE_THETA ** (jnp.arange(half, dtype=F32) * (2.0 / MLA_ROPE)))
    freq = jnp.concatenate([jnp.zeros((MLA_NOPE,), F32), inv_freq, inv_freq,
                            jnp.zeros((HEAD_PAD - MLA_NOPE - MLA_ROPE,), F32)])[None, :]
    pos = positions.astype(F32)[..., None]
    ts = TS_PREP
    return pl.pallas_call(
        _rope_kernel,
        grid=(bsz, seq // ts),
        in_specs=[pl.BlockSpec((1, ts, 1), lambda b, i: (b, i, 0)),
                  pl.BlockSpec((1, HEAD_PAD), lambda b, i: (0, 0))],
        out_specs=[pl.BlockSpec((1, ts, HEAD_PAD), lambda b, i: (b, i, 0))] * 2,
        out_shape=[jax.ShapeDtypeStruct((bsz, seq, HEAD_PAD), F32)] * 2,
        compiler_params=_params("parallel", "parallel"),
        name="rope_tables",
    )(pos, freq)


def _inproj_kernel(x_ref, w_ref, o_ref):
    xb = x_ref[...].astype(BF16)
    for c0 in range(0, N_MIX, 512):
        c1 = min(c0 + 512, N_MIX)
        o_ref[:, c0:c1] = jnp.dot(xb, w_ref[:, c0:c1], preferred_element_type=F32).astype(BF16)


def _inproj(x2, w_mix):
    n = x2.shape[0]
    tm = TM_PROJ
    return pl.pallas_call(
        _inproj_kernel,
        grid=(n // tm,),
        in_specs=[pl.BlockSpec((tm, D_MODEL), lambda i: (i, 0)),
                  pl.BlockSpec((D_MODEL, N_MIX), lambda i: (0, 0))],
        out_specs=pl.BlockSpec((tm, N_MIX), lambda i: (i, 0)),
        out_shape=jax.ShapeDtypeStruct((n, N_MIX), BF16),
        compiler_params=_params("parallel"),
        name="in_proj",
    )(x2, w_mix)


def _causal_taps(w_ref, buf, n_taps, ts):
    groups = {}
    for l in range(n_taps):
        off = CONV_HALO - (n_taps - 1) + l
        groups.setdefault(off % SUBLANES, []).append((l, off - off % SUBLANES))
    out = None
    for shift, taps in sorted(groups.items()):
        part = None
        for l, base in taps:
            term = w_ref[l:l + 1, :] * buf[base:base + ts + SUBLANES, :]
            part = term if part is None else part + term
        part = part[shift:shift + ts, :]
        out = part if out is None else out + part
    return out


def _conv_kernel(b_ref, c_ref, h_ref, v_ref, g_ref, ch_ref, hh_ref, vh_ref, gh_ref,
                 wsc_ref, wcf_ref, bcf_ref, lng_ref, lnb_ref, o0_ref, o1_ref, buf, *, ts):
    has_history = pl.program_id(1) > 0
    halo = CONV_HALO
    buf[halo + ts:, :] = jnp.zeros((SUBLANES, buf.shape[1]), F32)

    hist = ch_ref[0].astype(F32) * hh_ref[0].astype(F32)
    buf[0:halo, :] = jnp.where(has_history, hist, 0.0)
    buf[halo:halo + ts, :] = c_ref[0].astype(F32) * h_ref[0].astype(F32)
    acc = _causal_taps(wsc_ref, buf, SC_CONV_LEN, ts)
    o0_ref[0] = (b_ref[0].astype(F32) * acc).astype(BF16)

    hist = vh_ref[0].astype(F32) * _sigmoid(gh_ref[0].astype(F32))
    buf[0:halo, :] = jnp.where(has_history, hist, 0.0)
    buf[halo:halo + ts, :] = v_ref[0].astype(F32) * _sigmoid(g_ref[0].astype(F32))
    acc = _causal_taps(wcf_ref, buf, CF_CONV_LEN, ts)
    y = _layer_norm_rows(acc + bcf_ref[...], lng_ref[...], lnb_ref[...])
    o1_ref[0] = (y * _sigmoid(y)).astype(BF16)


def _conv_branches(proj3, sc_conv, cf_conv, cf_bias, cf_ln_g, cf_ln_b):
    bsz, seq, _ = proj3.shape
    ts = TS_CONV
    w = BRANCH_WIDTH
    hb = ts // CONV_HALO

    def main(col):
        return pl.BlockSpec((1, ts, w), lambda b, i: (b, i, col))

    def halo(col):
        return pl.BlockSpec((1, CONV_HALO, w), lambda b, i: (b, jnp.maximum(i * hb - 1, 0), col))

    def vec(rows):
        return pl.BlockSpec((rows, w), lambda b, i: (0, 0))

    return pl.pallas_call(
        functools.partial(_conv_kernel, ts=ts),
        grid=(bsz, seq // ts),
        in_specs=[main(OFF_SCB // w), main(OFF_SCC // w), main(OFF_SCH // w),
                  main(OFF_CFV // w), main(OFF_CFG // w),
                  halo(OFF_SCC // w), halo(OFF_SCH // w), halo(OFF_CFV // w), halo(OFF_CFG // w),
                  vec(SC_CONV_LEN), vec(CF_CONV_LEN), vec(1), vec(1), vec(1)],
        out_specs=[pl.BlockSpec((1, ts, w), lambda b, i: (b, i, 0))] * 2,
        out_shape=[jax.ShapeDtypeStruct((bsz, seq, w), BF16)] * 2,
        scratch_shapes=[pltpu.VMEM((CONV_HALO + ts + SUBLANES, w), F32)],
        compiler_params=_params("parallel", "parallel"),
        name="conv_branches",
    )(proj3, proj3, proj3, proj3, proj3, proj3, proj3, proj3, proj3,
      sc_conv, cf_conv, cf_bias[None, :], cf_ln_g[None, :], cf_ln_b[None, :])


def _mla_prep_kernel(cq_ref, ckv_ref, kr_ref, krr_ref, cos_ref, sin_ref, qn_ref, kvn_ref,
                     wq_ref, wqr_ref, wk_ref, wv_ref, q_ref, k_ref, v_ref):
    cos = cos_ref[0]
    sin = sin_ref[0]
    cos8 = jnp.concatenate([cos] * MLA_HEADS, axis=1)
    sin8 = jnp.concatenate([sin] * MLA_HEADS, axis=1)
    scale = (MLA_NOPE + MLA_ROPE) ** -0.5 * math.log2(math.e)

    cq = cq_ref[0].astype(F32)
    nq = cq * lax.rsqrt(jnp.mean(cq * cq, axis=-1, keepdims=True) + RMS_EPS) * qn_ref[...]
    nq = nq.astype(BF16)
    q1 = jnp.dot(nq, wq_ref[...], preferred_element_type=F32)
    q2 = jnp.dot(nq, wqr_ref[...], preferred_element_type=F32)
    q_ref[0] = ((q1 * cos8 + q2 * sin8) * scale).astype(BF16)

    ckv = ckv_ref[0].astype(F32)
    nkv = ckv * lax.rsqrt(jnp.mean(ckv * ckv, axis=-1, keepdims=True) + RMS_EPS) * kvn_ref[...]
    nkv = nkv.astype(BF16)
    k_rope = kr_ref[0].astype(F32) * cos + krr_ref[0].astype(F32) * sin
    k_nope = jnp.dot(nkv, wk_ref[...], preferred_element_type=F32)
    k_ref[0] = (k_nope + jnp.concatenate([k_rope] * MLA_HEADS, axis=1)).astype(BF16)
    v_ref[0] = jnp.dot(nkv, wv_ref[...], preferred_element_type=F32).astype(BF16)


def _mla_prep(proj3, cos, sin, q_norm, kv_norm, wq, wqr, wk, wv):
    bsz, seq, _ = proj3.shape
    ts = TS_PREP
    hw = MLA_HEADS * HEAD_PAD
    vw = MLA_HEADS * MLA_V

    def const(shape):
        return pl.BlockSpec(shape, lambda b, i: (0, 0))

    return pl.pallas_call(
        _mla_prep_kernel,
        grid=(bsz, seq // ts),
        in_specs=[pl.BlockSpec((1, ts, MLA_Q_RANK), lambda b, i: (b, i, OFF_CQ // MLA_Q_RANK)),
                  pl.BlockSpec((1, ts, MLA_KV_RANK), lambda b, i: (b, i, OFF_CKV // MLA_KV_RANK)),
                  pl.BlockSpec((1, ts, HEAD_PAD), lambda b, i: (b, i, OFF_KR // HEAD_PAD)),
                  pl.BlockSpec((1, ts, HEAD_PAD), lambda b, i: (b, i, OFF_KRR // HEAD_PAD)),
                  pl.BlockSpec((1, ts, HEAD_PAD), lambda b, i: (b, i, 0)),
                  pl.BlockSpec((1, ts, HEAD_PAD), lambda b, i: (b, i, 0)),
                  const((1, MLA_Q_RANK)), const((1, MLA_KV_RANK)),
                  const((MLA_Q_RANK, hw)), const((MLA_Q_RANK, hw)),
                  const((MLA_KV_RANK, hw)), const((MLA_KV_RANK, vw))],
        out_specs=[pl.BlockSpec((1, ts, hw), lambda b, i: (b, i, 0)),
                   pl.BlockSpec((1, ts, hw), lambda b, i: (b, i, 0)),
                   pl.BlockSpec((1, ts, vw), lambda b, i: (b, i, 0))],
        out_shape=[jax.ShapeDtypeStruct((bsz, seq, hw), BF16),
                   jax.ShapeDtypeStruct((bsz, seq, hw), BF16),
                   jax.ShapeDtypeStruct((bsz, seq, vw), BF16)],
        compiler_params=_params("parallel", "parallel"),
        name="mla_prep",
    )(proj3, proj3, proj3, proj3, cos, sin, q_norm[None, :], kv_norm[None, :], wq, wqr, wk, wv)


def _mla_attn_kernel(q_ref, k_ref, v_ref, o_ref, sa_ref, sb_ref, m_ref, acc_ref, *, t):
    qi = pl.program_id(2)
    row = lax.broadcasted_iota(jnp.int32, (t, t), 0)
    col = lax.broadcasted_iota(jnp.int32, (t, t), 1)
    allowed = (col // CHUNK) <= (row // CHUNK)
    ones = jnp.ones((t, LANES), BF16)
    m_ref[...] = jnp.full(m_ref.shape, -jnp.inf, F32)
    acc_ref[...] = jnp.zeros(acc_ref.shape, F32)

    def scores(kb, s_ref):
        kstart = pl.multiple_of(kb * t, t)
        for hh in range(2):
            lo = hh * HEAD_PAD
            s_ref[hh] = lax.dot_general(q_ref[0, :, lo:lo + HEAD_PAD],
                                        k_ref[0, pl.ds(kstart, t), lo:lo + HEAD_PAD], _NT,
                                        preferred_element_type=F32)

    def consume(kb, s_ref, masked):
        kstart = pl.multiple_of(kb * t, t)
        v_ext = jnp.concatenate([v_ref[0, pl.ds(kstart, t), :], ones], axis=1)
        for hh in range(2):
            s = s_ref[hh]
            if masked:
                s = jnp.where(allowed, s, -jnp.inf)
            m_prev = m_ref[hh]
            m_new = jnp.maximum(m_prev, jnp.max(s, axis=1, keepdims=True))
            alpha = jnp.exp2(m_prev - m_new)
            p = jnp.exp2(s - jnp.concatenate([m_new] * (t // LANES), axis=1))
            acc_ref[hh] = (jnp.concatenate([alpha, alpha], axis=1) * acc_ref[hh]
                           + jnp.dot(p.astype(BF16), v_ext, preferred_element_type=F32))
            m_ref[hh] = m_new

    scores(0, sa_ref)

    def body(j, carry):
        scores(2 * j + 1, sb_ref)
        consume(2 * j, sa_ref, False)
        scores(2 * j + 2, sa_ref)
        consume(2 * j + 1, sb_ref, False)
        return carry

    lax.fori_loop(0, lax.shift_right_logical(qi, 1), body, 0)

    @pl.when((qi & 1) == 0)
    def _():
        consume(qi, sa_ref, True)

    @pl.when((qi & 1) == 1)
    def _():
        scores(qi, sb_ref)
        consume(qi - 1, sa_ref, False)
        consume(qi, sb_ref, True)

    pv = 2 * MLA_V
    lane = lax.broadcasted_iota(jnp.int32, (t, pv), 1)
    out0 = acc_ref[0, :, :pv] / acc_ref[0, :, pv:]
    out1 = acc_ref[1, :, :pv] / acc_ref[1, :, pv:]
    o_ref[0] = jnp.where(lane < MLA_V, out0, out1).astype(BF16)


def _mla_attention(q, k, v):
    bsz, seq, _ = q.shape
    t = T_ATT
    pairs = MLA_HEADS // 2
    pv = 2 * MLA_V
    return pl.pallas_call(
        functools.partial(_mla_attn_kernel, t=t),
        grid=(bsz, pairs, seq // t),
        in_specs=[pl.BlockSpec((1, t, 2 * HEAD_PAD), lambda b, h, i: (b, i, h)),
                  pl.BlockSpec((1, seq, 2 * HEAD_PAD), lambda b, h, i: (b, 0, h)),
                  pl.BlockSpec((1, seq, pv), lambda b, h, i: (b, 0, h))],
        out_specs=pl.BlockSpec((1, t, pv), lambda b, h, i: (b, i, h)),
        out_shape=jax.ShapeDtypeStruct((bsz, seq, MLA_HEADS * MLA_V), BF16),
        scratch_shapes=[pltpu.VMEM((2, t, t), F32), pltpu.VMEM((2, t, t), F32),
                        pltpu.VMEM((2, t, LANES), F32),
                        pltpu.VMEM((2, t, pv + LANES), F32)],
        compiler_params=_params("parallel", "parallel", "arbitrary"),
        name="mla_attention",
    )(q, k, v)


def _sb_attn_kernel(q_ref, k_ref, v_ref, u_ref, o_ref, acc_ref, carry_ref, *, t, heads):
    qi = pl.program_id(2)
    row = lax.broadcasted_iota(jnp.int32, (t, t), 0)
    col = lax.broadcasted_iota(jnp.int32, (t, t), 1)
    before = col < row
    pw = 2 * SB_HEAD_DIM
    lane = lax.broadcasted_iota(jnp.int32, (t, pw), 1)
    reps = t // LANES
    acc_ref[...] = jnp.zeros(acc_ref.shape, F32)
    carry_ref[...] = jnp.zeros(carry_ref.shape, F32)

    def block(kstart, masked):
        for h in range(heads):
            lo = (h // 2) * pw
            q = q_ref[0, :, lo:lo + pw]
            in_head = (lane >= (h % 2) * SB_HEAD_DIM) & (lane < (h % 2 + 1) * SB_HEAD_DIM)
            qm = jnp.where(in_head, q, jnp.zeros_like(q))
            kblk = k_ref[0, pl.ds(kstart, t), lo:lo + pw]
            vblk = v_ref[0, pl.ds(kstart, t), lo:lo + pw]
            nz = lax.dot_general(qm, kblk, _NT, preferred_element_type=F32)
            log_stay = jnp.minimum(nz, 0.0) - jnp.log2(1.0 + jnp.exp2(-jnp.abs(nz)))
            if masked:
                log_stay = jnp.where(before, log_stay, 0.0)
            sums = jnp.dot(log_stay.astype(BF16), u_ref[...], preferred_element_type=F32)
            carry = carry_ref[h]
            own = log_stay - nz
            a = jnp.exp2(sums + (jnp.concatenate([carry] * reps, axis=1) + own))
            if masked:
                a = jnp.where(before, a, 0.0)
            acc_ref[h] += jnp.dot(a.astype(BF16), vblk, preferred_element_type=F32)
            carry_ref[h] = carry + jnp.sum(log_stay, axis=1, keepdims=True)

    block(pl.multiple_of(qi * t, t), True)

    def live():
        return jnp.max(carry_ref[...]) > SB_DEAD_LOG2

    def step(state):
        kb, _ = state
        block(pl.multiple_of(kb * t, t), False)
        return kb - 1, live()

    lax.while_loop(lambda st: (st[0] >= 0) & st[1], step, (qi - 1, live()))

    for pp in range(heads // 2):
        o_ref[0, :, pp * pw:(pp + 1) * pw] = jnp.where(
            lane < SB_HEAD_DIM, acc_ref[2 * pp], acc_ref[2 * pp + 1]).astype(BF16)


def _sb_attention(proj3):
    bsz, seq, _ = proj3.shape
    t = T_SB
    heads = SB_HEADS_PER_STEP
    w = heads * SB_HEAD_DIM
    r = lax.broadcasted_iota(jnp.int32, (t, t), 0)
    c = lax.broadcasted_iota(jnp.int32, (t, t), 1)
    suffix = (r > c).astype(BF16)
    return pl.pallas_call(
        functools.partial(_sb_attn_kernel, t=t, heads=heads),
        grid=(bsz, SB_HEADS // heads, seq // t),
        in_specs=[pl.BlockSpec((1, t, w), lambda b, h, i: (b, i, OFF_SBQ // w + h)),
                  pl.BlockSpec((1, seq, w), lambda b, h, i: (b, 0, OFF_SBK // w + h)),
                  pl.BlockSpec((1, seq, w), lambda b, h, i: (b, 0, OFF_SBV // w + h)),
                  pl.BlockSpec((t, t), lambda b, h, i: (0, 0))],
        out_specs=pl.BlockSpec((1, t, w), lambda b, h, i: (b, i, h)),
        out_shape=jax.ShapeDtypeStruct((bsz, seq, SB_HEADS * SB_HEAD_DIM), BF16),
        scratch_shapes=[pltpu.VMEM((heads, t, 2 * SB_HEAD_DIM), F32), pltpu.VMEM((heads, t, LANES), F32)],
        compiler_params=_params("parallel", "parallel", "arbitrary"),
        name="sb_attention",
    )(proj3, proj3, proj3, suffix)


def _merge_kernel(x_ref, b0_ref, b1_ref, b2_ref, b3_ref, wg_ref, bg_ref, wb_ref, wo_ref,
                  lg_ref, lb_ref, o_ref):
    rows = x_ref.shape[0] // MERGE_ROW_CHUNKS
    for c in range(MERGE_ROW_CHUNKS):
        sl = slice(c * rows, (c + 1) * rows)
        x = x_ref[sl, :]
        xb = x.astype(BF16)
        merged = None
        for n, br in enumerate((b0_ref, b1_ref, b2_ref, b3_ref)):
            c0 = n * D_MODEL
            pre = jnp.dot(xb, wg_ref[:, c0:c0 + D_MODEL], preferred_element_type=F32)
            gate = _sigmoid(pre + bg_ref[:, c0:c0 + D_MODEL])
            term = gate * jnp.dot(br[sl, :], wb_ref[n], preferred_element_type=F32)
            merged = term if merged is None else merged + term
        y = jnp.dot(merged.astype(BF16), wo_ref[...], preferred_element_type=F32)
        o_ref[sl, :] = _layer_norm_rows(DN_ALPHA * x + y, lg_ref[...], lb_ref[...])


def _merge(x2, branches, w_gate, b_gate, w_branch, w_out, ln_g, ln_b):
    n = x2.shape[0]
    tm = TM_MERGE
    nb = len(branches)

    def const2(shape):
        return pl.BlockSpec(shape, lambda i: (0, 0))

    return pl.pallas_call(
        _merge_kernel,
        grid=(n // tm,),
        in_specs=[pl.BlockSpec((tm, D_MODEL), lambda i: (i, 0))]
                 + [pl.BlockSpec((tm, BRANCH_WIDTH), lambda i: (i, 0))] * nb
                 + [const2((D_MODEL, nb * D_MODEL)), const2((1, nb * D_MODEL)),
                    pl.BlockSpec((nb, BRANCH_WIDTH, D_MODEL), lambda i: (0, 0, 0)),
                    const2((D_MODEL, D_MODEL)), const2((1, D_MODEL)), const2((1, D_MODEL))],
        out_specs=pl.BlockSpec((tm, D_MODEL), lambda i: (i, 0)),
        out_shape=jax.ShapeDtypeStruct((n, D_MODEL), F32),
        compiler_params=_params("parallel"),
        name="merge_out",
    )(x2, *branches, w_gate, b_gate[None, :], w_branch, w_out, ln_g[None, :], ln_b[None, :])


def _swiglu_rows(x_rows, wg, wu, wd):
    xb = x_rows.astype(BF16)
    g = jnp.dot(xb, wg, preferred_element_type=F32)
    u = jnp.dot(xb, wu, preferred_element_type=F32)
    h = (g * _sigmoid(g) * u).astype(BF16)
    return jnp.dot(h, wd, preferred_element_type=F32)


def _ffn_accumulate(x_ref, wg, wu, wd, acc_ref, j, finish):
    rows = x_ref.shape[0] // FFN_ROW_CHUNKS
    for c in range(FFN_ROW_CHUNKS):
        sl = slice(c * rows, (c + 1) * rows)
        x_rows = x_ref[sl, :]
        acc = jnp.where(j > 0, acc_ref[sl, :], 0.0) + _swiglu_rows(x_rows, wg, wu, wd)
        acc_ref[sl, :] = acc
        finish(sl, x_rows, acc)


def _dense_ffn_kernel(x_ref, wg_ref, wu_ref, wd_ref, lg_ref, lb_ref, o_ref, acc_ref):
    j = pl.program_id(1)

    @pl.when((pl.program_id(0) == 0) & (j == 0))
    def _():
        acc_ref[...] = jnp.zeros(acc_ref.shape, F32)

    def finish(sl, x_rows, acc):
        o_ref[sl, :] = _layer_norm_rows(DN_ALPHA * x_rows + acc, lg_ref[...], lb_ref[...])

    _ffn_accumulate(x_ref, wg_ref[...], wu_ref[...], wd_ref[...], acc_ref, j, finish)


def _dense_ffn(x2, wg, wu, wd, ln_g, ln_b):
    n = x2.shape[0]
    tm, tf = TM_FFN, TF_FFN
    return pl.pallas_call(
        _dense_ffn_kernel,
        grid=(n // tm, D_FF // tf),
        in_specs=[pl.BlockSpec((tm, D_MODEL), lambda i, j: (i, 0)),
                  pl.BlockSpec((D_MODEL, tf), lambda i, j: (0, j)),
                  pl.BlockSpec((D_MODEL, tf), lambda i, j: (0, j)),
                  pl.BlockSpec((tf, D_MODEL), lambda i, j: (j, 0)),
                  pl.BlockSpec((1, D_MODEL), lambda i, j: (0, 0)),
                  pl.BlockSpec((1, D_MODEL), lambda i, j: (0, 0))],
        out_specs=pl.BlockSpec((tm, D_MODEL), lambda i, j: (i, 0)),
        out_shape=jax.ShapeDtypeStruct((n, D_MODEL), F32),
        scratch_shapes=[pltpu.VMEM((tm, D_MODEL), F32)],
        compiler_params=_params("parallel", "arbitrary"),
        name="dense_ffn",
    )(x2, wg, wu, wd, ln_g[None, :], ln_b[None, :])


def _router_kernel(x_ref, wr_ref, tri_ref, meta_ref, cnt_ref, run_ref):
    @pl.when(pl.program_id(0) == 0)
    def _():
        run_ref[...] = jnp.zeros(run_ref.shape, F32)

    x = x_ref[...]
    w = wr_ref[...]
    x_hi = x.astype(BF16)
    x_lo = (x - x_hi.astype(F32)).astype(BF16)
    w_hi = w.astype(BF16)
    w_lo = (w - w_hi.astype(F32)).astype(BF16)
    logits = (jnp.dot(x_hi, w_hi, preferred_element_type=F32)
              + jnp.dot(x_lo, w_hi, preferred_element_type=F32)
              + jnp.dot(x_hi, w_lo, preferred_element_type=F32))
    lane = lax.broadcasted_iota(jnp.int32, logits.shape, 1)
    logits = jnp.where(lane < N_EXPERTS, logits, -jnp.inf)
    v1 = jnp.max(logits, axis=1, keepdims=True)
    i1 = jnp.min(jnp.where(logits == v1, lane, LANES), axis=1, keepdims=True)
    rest = jnp.where(lane == i1, -jnp.inf, logits)
    v2 = jnp.max(rest, axis=1, keepdims=True)
    i2 = jnp.min(jnp.where(rest == v2, lane, LANES), axis=1, keepdims=True)
    ex = jnp.exp(v2 - v1)
    den = 1.0 + ex
    w1 = 1.0 / den
    w2 = ex / den

    sel1 = lane == i1
    sel2 = lane == i2
    chosen = jnp.where(sel1 | sel2, 1.0, 0.0)
    run = run_ref[...]
    before = jnp.dot(tri_ref[...], chosen.astype(BF16), preferred_element_type=F32) + run
    r1 = jnp.sum(jnp.where(sel1, before, 0.0), axis=1, keepdims=True)
    r2 = jnp.sum(jnp.where(sel2, before, 0.0), axis=1, keepdims=True)
    run = run + jnp.sum(chosen, axis=0, keepdims=True)
    run_ref[...] = run
    cnt_ref[...] = run

    meta = jnp.where(lane == 0, i1.astype(F32), 0.0)
    meta = jnp.where(lane == 1, i2.astype(F32), meta)
    meta = jnp.where(lane == 2, w1, meta)
    meta = jnp.where(lane == 3, w2, meta)
    meta = jnp.where(lane == 4, r1, meta)
    meta = jnp.where(lane == 5, r2, meta)
    meta_ref[...] = meta


def _router(x2, w_router):
    n = x2.shape[0]
    tm = TM_ROUTE
    wr = jnp.pad(w_router, ((0, 0), (0, LANES - N_EXPERTS)))
    r = lax.broadcasted_iota(jnp.int32, (tm, tm), 0)
    c = lax.broadcasted_iota(jnp.int32, (tm, tm), 1)
    tri = (c < r).astype(BF16)
    return pl.pallas_call(
        _router_kernel,
        grid=(n // tm,),
        in_specs=[pl.BlockSpec((tm, D_MODEL), lambda i: (i, 0)),
                  pl.BlockSpec((D_MODEL, LANES), lambda i: (0, 0)),
                  pl.BlockSpec((tm, tm), lambda i: (0, 0))],
        out_specs=[pl.BlockSpec((tm, LANES), lambda i: (i, 0)),
                   pl.BlockSpec((1, LANES), lambda i: (0, 0))],
        out_shape=[jax.ShapeDtypeStruct((n, LANES), F32),
                   jax.ShapeDtypeStruct((1, LANES), F32)],
        scratch_shapes=[pltpu.VMEM((1, LANES), F32)],
        compiler_params=_params("arbitrary"),
        name="router",
    )(x2, wr, tri)


def _row_copy(src_hbm, dst_ref, src_row, dst_row, sem):
    return pltpu.make_async_copy(src_hbm.at[pl.ds(src_row, 1)], dst_ref.at[pl.ds(dst_row, 1)], sem)


def _all_rows_copy(src_hbm, dst_ref, sem):
    return pltpu.make_async_copy(src_hbm.at[pl.ds(0, dst_ref.shape[0])], dst_ref, sem)


def _dispatch_kernel(p1_ref, p2_ref, x_ref, init_hbm, o_hbm, sem, *, tt):
    del init_hbm

    def chunk(c, carry):
        for u in range(DMA_UNROLL):
            r = c * DMA_UNROLL + u
            src = x_ref.at[pl.ds(r, 1)]
            pltpu.make_async_copy(src, o_hbm.at[pl.ds(p1_ref[0, 0, r], 1)], sem.at[0]).start(priority=0)
            pltpu.make_async_copy(src, o_hbm.at[pl.ds(p2_ref[0, 0, r], 1)], sem.at[1]).start(priority=1)
        return carry

    lax.fori_loop(0, tt // DMA_UNROLL, chunk, 0)
    pltpu.make_async_copy(x_ref, o_hbm.at[pl.ds(0, tt)], sem.at[0]).wait()
    pltpu.make_async_copy(x_ref, o_hbm.at[pl.ds(0, tt)], sem.at[1]).wait()


def _dispatch_rows(x2, pos1, pos2, p):
    n = x2.shape[0]
    tt = TT_DISPATCH
    idx = pl.BlockSpec((1, 1, tt), lambda i: (i, 0, 0), memory_space=pltpu.SMEM)
    return pl.pallas_call(
        functools.partial(_dispatch_kernel, tt=tt),
        grid=(n // tt,),
        in_specs=[idx, idx,
                  pl.BlockSpec((tt, D_MODEL), lambda i: (i, 0)),
                  pl.BlockSpec(memory_space=pl.ANY)],
        out_specs=pl.BlockSpec(memory_space=pl.ANY),
        out_shape=jax.ShapeDtypeStruct((p, D_MODEL), x2.dtype),
        input_output_aliases={3: 0},
        scratch_shapes=[pltpu.SemaphoreType.DMA((2,))],
        compiler_params=_params("arbitrary"),
        name="moe_dispatch",
    )(pos1.reshape(n // tt, 1, tt), pos2.reshape(n // tt, 1, tt), x2, jnp.zeros((p, D_MODEL), x2.dtype))


def _moe_ffn_kernel(te_ref, nt_ref, x_ref, wg_ref, wu_ref, wd_ref, o_ref, acc_ref):
    i = pl.program_id(0)
    j = pl.program_id(1)

    @pl.when((i == 0) & (j == 0))
    def _():
        acc_ref[...] = jnp.zeros(acc_ref.shape, F32)

    @pl.when(i < nt_ref[0])
    def _():
        def finish(sl, x_rows, acc):
            o_ref[sl, :] = acc

        _ffn_accumulate(x_ref, wg_ref[0], wu_ref[0], wd_ref[0], acc_ref, j, finish)

    @pl.when(i >= nt_ref[0])
    def _():
        o_ref[...] = jnp.zeros(o_ref.shape, o_ref.dtype)


def _moe_ffn(xs, tile_expert, num_tiles, wg, wu, wd):
    p = xs.shape[0]
    tm, tf = TM_FFN, TF_FFN
    nj = D_FF // tf

    def row(i, j, te, nt):
        return (jnp.minimum(i, nt[0] - 1), 0)

    def col_j(i, j, nt):
        return jnp.where(i < nt[0], j, nj - 1)

    def w_up(i, j, te, nt):
        return (te[jnp.minimum(i, nt[0] - 1)], 0, col_j(i, j, nt))

    def w_down(i, j, te, nt):
        return (te[jnp.minimum(i, nt[0] - 1)], col_j(i, j, nt), 0)

    grid_spec = pltpu.PrefetchScalarGridSpec(
        num_scalar_prefetch=2,
        grid=(p // tm, nj),
        in_specs=[pl.BlockSpec((tm, D_MODEL), row),
                  pl.BlockSpec((1, D_MODEL, tf), w_up),
                  pl.BlockSpec((1, D_MODEL, tf), w_up),
                  pl.BlockSpec((1, tf, D_MODEL), w_down)],
        out_specs=pl.BlockSpec((tm, D_MODEL), lambda i, j, te, nt: (i, 0)),
        scratch_shapes=[pltpu.VMEM((tm, D_MODEL), F32)],
    )
    return pl.pallas_call(
        _moe_ffn_kernel,
        grid_spec=grid_spec,
        out_shape=jax.ShapeDtypeStruct((p, D_MODEL), F32),
        compiler_params=_params("arbitrary", "arbitrary"),
        name="moe_ffn",
    )(tile_expert, num_tiles, xs, wg, wu, wd)


def _combine_kernel(p1_ref, p2_ref, n1_ref, n2_ref, x_ref, w1_ref, w2_ref, y_hbm, lg_ref, lb_ref, o_ref,
                    buf1, buf2, sem, *, tc):
    i = pl.program_id(0)
    slot = i & 1

    def fetch(a_ref, b_ref, s):
        def chunk(c, carry):
            for u in range(DMA_UNROLL):
                r = c * DMA_UNROLL + u
                _row_copy(y_hbm, buf1.at[s], a_ref[0, 0, r], r, sem.at[0, s]).start(priority=0)
                _row_copy(y_hbm, buf2.at[s], b_ref[0, 0, r], r, sem.at[1, s]).start(priority=1)
            return carry

        lax.fori_loop(0, tc // DMA_UNROLL, chunk, 0)

    @pl.when(i == 0)
    def _():
        fetch(p1_ref, p2_ref, 0)

    @pl.when(i + 1 < pl.num_programs(0))
    def _():
        fetch(n1_ref, n2_ref, 1 - slot)

    _all_rows_copy(y_hbm, buf1.at[slot], sem.at[0, slot]).wait()
    _all_rows_copy(y_hbm, buf2.at[slot], sem.at[1, slot]).wait()
    f = w1_ref[...] * buf1[slot] + w2_ref[...] * buf2[slot]
    o_ref[...] = _layer_norm_rows(DN_ALPHA * x_ref[...] + f, lg_ref[...], lb_ref[...])


def _combine(x2, y, pos1, pos2, w1, w2, ln_g, ln_b):
    n = x2.shape[0]
    tc = TC_COMBINE
    steps = n // tc
    idx = pl.BlockSpec((1, 1, tc), lambda i: (i, 0, 0), memory_space=pltpu.SMEM)
    nxt = pl.BlockSpec((1, 1, tc), lambda i: (jnp.minimum(i + 1, steps - 1), 0, 0), memory_space=pltpu.SMEM)
    p1 = pos1.reshape(steps, 1, tc)
    p2 = pos2.reshape(steps, 1, tc)
    return pl.pallas_call(
        functools.partial(_combine_kernel, tc=tc),
        grid=(steps,),
        in_specs=[idx, idx, nxt, nxt,
                  pl.BlockSpec((tc, D_MODEL), lambda i: (i, 0)),
                  pl.BlockSpec((tc, 1), lambda i: (i, 0)),
                  pl.BlockSpec((tc, 1), lambda i: (i, 0)),
                  pl.BlockSpec(memory_space=pl.ANY),
                  pl.BlockSpec((1, D_MODEL), lambda i: (0, 0)),
                  pl.BlockSpec((1, D_MODEL), lambda i: (0, 0))],
        out_specs=pl.BlockSpec((tc, D_MODEL), lambda i: (i, 0)),
        out_shape=jax.ShapeDtypeStruct((n, D_MODEL), F32),
        scratch_shapes=[pltpu.VMEM((2, tc, D_MODEL), F32), pltpu.VMEM((2, tc, D_MODEL), F32),
                        pltpu.SemaphoreType.DMA((2, 2))],
        compiler_params=_params("arbitrary"),
        name="moe_combine",
    )(p1, p2, p1, p2, x2, w1, w2, y, ln_g[None, :], ln_b[None, :])


def _moe_layer(x2, w_router, wg, wu, wd, ln_g, ln_b):
    n = x2.shape[0]
    tm = TM_FFN
    n_tiles = (2 * n) // tm + N_EXPERTS
    p = n_tiles * tm

    meta, counts = _router(x2, w_router)
    e1 = meta[:, 0].astype(jnp.int32)
    e2 = meta[:, 1].astype(jnp.int32)
    w1 = meta[:, 2:3]
    w2 = meta[:, 3:4]
    r1 = meta[:, 4].astype(jnp.int32)
    r2 = meta[:, 5].astype(jnp.int32)
    cnt = counts[0, :N_EXPERTS].astype(jnp.int32)

    tiles_per = (cnt + tm - 1) // tm
    tile_end = jnp.cumsum(tiles_per)
    start = (tile_end - tiles_per) * tm
    pos1 = start[e1] + r1
    pos2 = start[e2] + r2
    num_tiles = tile_end[-1:]
    tile_expert = jnp.minimum(
        jnp.sum((jnp.arange(n_tiles)[:, None] >= tile_end[None, :]).astype(jnp.int32), axis=1),
        N_EXPERTS - 1)
    xs = _dispatch_rows(x2, pos1, pos2, p)
    y = _moe_ffn(xs, tile_expert, num_tiles, wg, wu, wd)
    return _combine(x2, y, pos1, pos2, w1, w2, ln_g, ln_b)


def _prep_in_proj(w):
    sc = w[:, 0:1536]
    cf = w[:, 1536:2560]
    cq = w[:, 2560:2816]
    ckv = w[:, 2816:2944]
    kr = w[:, 2944:2976]
    sb = w[:, 2976:4512]
    gate = w[:, 4512:]
    half = MLA_ROPE // 2
    z_nope = jnp.zeros((D_MODEL, MLA_NOPE), F32)
    z_tail = jnp.zeros((D_MODEL, HEAD_PAD - MLA_NOPE - MLA_ROPE), F32)
    kr_pad = jnp.concatenate([z_nope, kr, z_tail], axis=1)
    kr_rot = jnp.concatenate([z_nope, -kr[:, half:], kr[:, :half], z_tail], axis=1)
    hw = SB_HEADS * SB_HEAD_DIM
    sbq = sb[:, :hw] * (-(SB_HEAD_DIM ** -0.5) * math.log2(math.e))
    mix = jnp.concatenate([sc, cf, sbq, sb[:, hw:], cq, ckv, kr_pad, kr_rot], axis=1)
    return mix.astype(BF16), gate.astype(BF16)


def _prep_mla(w_uq, w_ukv):
    half = MLA_ROPE // 2
    wq = w_uq.reshape(MLA_Q_RANK, MLA_HEADS, MLA_NOPE + MLA_ROPE)
    nope, rope = wq[..., :MLA_NOPE], wq[..., MLA_NOPE:]
    tail = jnp.zeros((MLA_Q_RANK, MLA_HEADS, HEAD_PAD - MLA_NOPE - MLA_ROPE), F32)
    q_main = jnp.concatenate([nope, rope, tail], axis=-1)
    q_rot = jnp.concatenate([jnp.zeros_like(nope), -rope[..., half:], rope[..., :half], tail], axis=-1)
    wkv = w_ukv.reshape(MLA_KV_RANK, MLA_HEADS, MLA_NOPE + MLA_V)
    k_nope, v = wkv[..., :MLA_NOPE], wkv[..., MLA_NOPE:]
    k_main = jnp.concatenate(
        [k_nope, jnp.zeros((MLA_KV_RANK, MLA_HEADS, HEAD_PAD - MLA_NOPE), F32)], axis=-1)
    hw = MLA_HEADS * HEAD_PAD
    return (q_main.reshape(MLA_Q_RANK, hw).astype(BF16), q_rot.reshape(MLA_Q_RANK, hw).astype(BF16),
            k_main.reshape(MLA_KV_RANK, hw).astype(BF16),
            v.reshape(MLA_KV_RANK, MLA_HEADS * MLA_V).astype(BF16))


def kernel(x, positions, w_in, b_gate, sc_conv, cf_conv, cf_conv_bias, cf_ln_g, cf_ln_b,
           mla_q_norm, mla_w_uq, mla_kv_norm, mla_w_ukv, w_branch, w_out,
           ln_mix_g, ln_mix_b, ln_ffn_g, ln_ffn_b, ffn_w_gate, ffn_w_up, ffn_w_down,
           router_w, exp_w_gate, exp_w_up, exp_w_down):
    bsz, seq, d = x.shape
    n = bsz * seq
    cos, sin = _rope_tables(positions)
    x2 = x.reshape(n, d)
    for layer in range(DEPTH):
        w_mix, w_gate = _prep_in_proj(w_in[layer])
        wq, wqr, wk, wv = _prep_mla(mla_w_uq[layer], mla_w_ukv[layer])
        proj = _inproj(x2, w_mix).reshape(bsz, seq, N_MIX)
        br0, br1 = _conv_branches(proj, sc_conv[layer], cf_conv[layer], cf_conv_bias[layer],
                                  cf_ln_g[layer], cf_ln_b[layer])
        q, k, v = _mla_prep(proj, cos, sin, mla_q_norm[layer], mla_kv_norm[layer], wq, wqr, wk, wv)
        br2 = _mla_attention(q, k, v)
        br3 = _sb_attention(proj)
        branches = [b.reshape(n, BRANCH_WIDTH) for b in (br0, br1, br2, br3)]
        x2 = _merge(x2, branches, w_gate, b_gate[layer], w_branch[layer].astype(BF16),
                    w_out[layer].astype(BF16), ln_mix_g[layer], ln_mix_b[layer])
        i = layer // 2
        if layer % 2 == 0:
            x2 = _dense_ffn(x2, ffn_w_gate[i].astype(BF16), ffn_w_up[i].astype(BF16),
                            ffn_w_down[i].astype(BF16), ln_ffn_g[layer], ln_ffn_b[layer])
        else:
            x2 = _moe_layer(x2, router_w[i], exp_w_gate[i].astype(BF16), exp_w_up[i].astype(BF16),
                            exp_w_down[i].astype(BF16), ln_ffn_g[layer], ln_ffn_b[layer])
    return x2.reshape(bsz, seq, d)
```

```python
import functools
import math

import jax
import jax.numpy as jnp
from jax import lax
from jax.experimental import pallas as pl
from jax.experimental.pallas import tpu as pltpu

F32 = jnp.float32
BF16 = jnp.bfloat16

D_MODEL = 1024
DEPTH = 2
CHUNK = 64
BRANCH_WIDTH = 512
SC_CONV_LEN = 3
CF_CONV_LEN = 31
MLA_HEADS = 8
MLA_NOPE = 64
MLA_ROPE = 32
MLA_V = 64
MLA_Q_RANK = 256
MLA_KV_RANK = 128
ROPE_THETA = 10000.0
SB_HEADS = 8
SB_HEAD_DIM = 64
D_FF = 3584
N_EXPERTS = 8
DN_ALPHA = (2 * DEPTH) ** 0.25
LN_EPS = 1e-5
RMS_EPS = 1e-6
SB_DEAD_LOG2 = -150.0

LANES = 128
SUBLANES = 8
HEAD_PAD = 128
CONV_HALO = 32
VMEM_LIMIT = 56 * 1024 * 1024

OFF_SCB, OFF_SCC, OFF_SCH = 0, 512, 1024
OFF_CFV, OFF_CFG = 1536, 2048
N_CONV_IN = 2560
OFF_SBQ, OFF_SBK, OFF_SBV = 0, 512, 1024
OFF_CQ = 1536
OFF_CKV = 1792
OFF_KR = 1920
OFF_KRR = 2048
N_OUT = 2176
N_MIX = N_CONV_IN + N_OUT

TM_PROJ = 1024
PROJ_SUB_ROWS = 256
TS_PREP = 512
T_ATT = 512
T_SB = 256
SB_HEADS_PER_STEP = 8
TM_MERGE = 512
TM_FFN = 512
TF_FFN = 1792
FFN_ROW_CHUNKS = 2
MERGE_ROW_CHUNKS = 2
TM_ROUTE = 512
TT_DISPATCH = 512
TC_COMBINE = 256
DMA_UNROLL = 8

_NT = (((1,), (1,)), ((), ()))


def _params(*sem):
    return pltpu.CompilerParams(dimension_semantics=sem, vmem_limit_bytes=VMEM_LIMIT)


def _sigmoid(x):
    return 1.0 / (1.0 + jnp.exp(-x))


def _layer_norm_rows(v, g, b):
    mu = jnp.mean(v, axis=-1, keepdims=True)
    c = v - mu
    var = jnp.mean(c * c, axis=-1, keepdims=True)
    return c * lax.rsqrt(var + LN_EPS) * g + b


def _rope_kernel(pos_ref, freq_ref, cos_ref, sin_ref):
    ang = pos_ref[0] * freq_ref[...]
    cos_ref[0] = jnp.cos(ang)
    sin_ref[0] = jnp.sin(ang)


def _rope_tables(positions):
    bsz, seq = positions.shape
    half = MLA_ROPE // 2
    inv_freq = 1.0 / (ROPE_THETA ** (jnp.arange(half, dtype=F32) * (2.0 / MLA_ROPE)))
    freq = jnp.concatenate([jnp.zeros((MLA_NOPE,), F32), inv_freq, inv_freq,
                            jnp.zeros((HEAD_PAD - MLA_NOPE - MLA_ROPE,), F32)])[None, :]
    pos = positions.astype(F32)[..., None]
    ts = TS_PREP
    return pl.pallas_call(
        _rope_kernel,
        grid=(bsz, seq // ts),
        in_specs=[pl.BlockSpec((1, ts, 1), lambda b, i: (b, i, 0)),
                  pl.BlockSpec((1, HEAD_PAD), lambda b, i: (0, 0))],
        out_specs=[pl.BlockSpec((1, ts, HEAD_PAD), lambda b, i: (b, i, 0))] * 2,
        out_shape=[jax.ShapeDtypeStruct((bsz, seq, HEAD_PAD), F32)] * 2,
        compiler_params=_params("parallel", "parallel"),
        name="rope_tables",
    )(pos, freq)


def _causal_taps(w_ref, buf, n_taps, base, rows):
    groups = {}
    for l in range(n_taps):
        off = CONV_HALO - (n_taps - 1) + l
        groups.setdefault(off % SUBLANES, []).append((l, off - off % SUBLANES))
    out = None
    for shift, taps in sorted(groups.items()):
        part = None
        for l, lo in taps:
            term = w_ref[l:l + 1, :] * buf[base + lo:base + lo + rows + SUBLANES, :]
            part = term if part is None else part + term
        part = part[shift:shift + rows, :]
        out = part if out is None else out + part
    return out


def _inproj_conv_kernel(x_ref, w_ref, wsc_ref, wcf_ref, bcf_ref, lng_ref, lnb_ref,
                        o_ref, o0_ref, o1_ref, bufc, bufu, *, tm, steps_per_seq):
    i = pl.program_id(0)
    halo = CONV_HALO
    w = BRANCH_WIDTH

    @pl.when(i == 0)
    def _():
        bufc[...] = jnp.zeros(bufc.shape, F32)
        bufu[...] = jnp.zeros(bufu.shape, F32)

    starts_sequence = (i % steps_per_seq) == 0
    bufc[0:halo, :] = jnp.where(starts_sequence, 0.0, bufc[tm:tm + halo, :])
    bufu[0:halo, :] = jnp.where(starts_sequence, 0.0, bufu[tm:tm + halo, :])

    def proj(xb, c0, width):
        return jnp.dot(xb, w_ref[:, c0:c0 + width], preferred_element_type=F32)

    rs = PROJ_SUB_ROWS
    for s0 in range(0, tm, rs):
        rows = slice(s0, s0 + rs)
        xb = x_ref[rows, :].astype(BF16)
        gate_b = proj(xb, OFF_SCB, w)
        bufc[halo + s0:halo + s0 + rs, :] = proj(xb, OFF_SCC, w) * proj(xb, OFF_SCH, w)
        bufu[halo + s0:halo + s0 + rs, :] = proj(xb, OFF_CFV, w) * _sigmoid(proj(xb, OFF_CFG, w))
        for c0 in range(0, N_OUT, 512):
            c1 = min(c0 + 512, N_OUT)
            o_ref[rows, c0:c1] = proj(xb, N_CONV_IN + c0, c1 - c0).astype(BF16)
        o0_ref[rows, :] = (gate_b * _causal_taps(wsc_ref, bufc, SC_CONV_LEN, s0, rs)).astype(BF16)
        y = _layer_norm_rows(_causal_taps(wcf_ref, bufu, CF_CONV_LEN, s0, rs) + bcf_ref[...],
                             lng_ref[...], lnb_ref[...])
        o1_ref[rows, :] = (y * _sigmoid(y)).astype(BF16)


def _inproj_conv(x2, w_mix, seq, sc_conv, cf_conv, cf_bias, cf_ln_g, cf_ln_b):
    n = x2.shape[0]
    tm = TM_PROJ
    w = BRANCH_WIDTH

    def const(shape):
        return pl.BlockSpec(shape, lambda i: (0, 0))

    return pl.pallas_call(
        functools.partial(_inproj_conv_kernel, tm=tm, steps_per_seq=seq // tm),
        grid=(n // tm,),
        in_specs=[pl.BlockSpec((tm, D_MODEL), lambda i: (i, 0)),
                  const((D_MODEL, N_MIX)),
                  const((SC_CONV_LEN, w)), const((CF_CONV_LEN, w)), const((1, w)), const((1, w)), const((1, w))],
        out_specs=[pl.BlockSpec((tm, N_OUT), lambda i: (i, 0)),
                   pl.BlockSpec((tm, w), lambda i: (i, 0)),
                   pl.BlockSpec((tm, w), lambda i: (i, 0))],
        out_shape=[jax.ShapeDtypeStruct((n, N_OUT), BF16),
                   jax.ShapeDtypeStruct((n, w), BF16),
                   jax.ShapeDtypeStruct((n, w), BF16)],
        scratch_shapes=[pltpu.VMEM((CONV_HALO + tm + SUBLANES, w), F32),
                        pltpu.VMEM((CONV_HALO + tm + SUBLANES, w), F32)],
        compiler_params=_params("arbitrary"),
        name="in_proj_conv",
    )(x2, w_mix, sc_conv, cf_conv, cf_bias[None, :], cf_ln_g[None, :], cf_ln_b[None, :])


def _mla_prep_kernel(cq_ref, ckv_ref, kr_ref, krr_ref, cos_ref, sin_ref, qn_ref, kvn_ref,
                     wq_ref, wqr_ref, wk_ref, wv_ref, q_ref, k_ref, v_ref):
    cos = cos_ref[0]
    sin = sin_ref[0]
    cos8 = jnp.concatenate([cos] * MLA_HEADS, axis=1)
    sin8 = jnp.concatenate([sin] * MLA_HEADS, axis=1)
    scale = (MLA_NOPE + MLA_ROPE) ** -0.5 * math.log2(math.e)

    cq = cq_ref[0].astype(F32)
    nq = cq * lax.rsqrt(jnp.mean(cq * cq, axis=-1, keepdims=True) + RMS_EPS) * qn_ref[...]
    nq = nq.astype(BF16)
    q1 = jnp.dot(nq, wq_ref[...], preferred_element_type=F32)
    q2 = jnp.dot(nq, wqr_ref[...], preferred_element_type=F32)
    q_ref[0] = ((q1 * cos8 + q2 * sin8) * scale).astype(BF16)

    ckv = ckv_ref[0].astype(F32)
    nkv = ckv * lax.rsqrt(jnp.mean(ckv * ckv, axis=-1, keepdims=True) + RMS_EPS) * kvn_ref[...]
    nkv = nkv.astype(BF16)
    k_rope = kr_ref[0].astype(F32) * cos + krr_ref[0].astype(F32) * sin
    k_nope = jnp.dot(nkv, wk_ref[...], preferred_element_type=F32)
    k_ref[0] = (k_nope + jnp.concatenate([k_rope] * MLA_HEADS, axis=1)).astype(BF16)
    v_ref[0] = jnp.dot(nkv, wv_ref[...], preferred_element_type=F32).astype(BF16)


def _mla_prep(proj3, cos, sin, q_norm, kv_norm, wq, wqr, wk, wv):
    bsz, seq, _ = proj3.shape
    ts = TS_PREP
    hw = MLA_HEADS * HEAD_PAD
    vw = MLA_HEADS * MLA_V

    def const(shape):
        return pl.BlockSpec(shape, lambda b, i: (0, 0))

    return pl.pallas_call(
        _mla_prep_kernel,
        grid=(bsz, seq // ts),
        in_specs=[pl.BlockSpec((1, ts, MLA_Q_RANK), lambda b, i: (b, i, OFF_CQ // MLA_Q_RANK)),
                  pl.BlockSpec((1, ts, MLA_KV_RANK), lambda b, i: (b, i, OFF_CKV // MLA_KV_RANK)),
                  pl.BlockSpec((1, ts, HEAD_PAD), lambda b, i: (b, i, OFF_KR // HEAD_PAD)),
                  pl.BlockSpec((1, ts, HEAD_PAD), lambda b, i: (b, i, OFF_KRR // HEAD_PAD)),
                  pl.BlockSpec((1, ts, HEAD_PAD), lambda b, i: (b, i, 0)),
                  pl.BlockSpec((1, ts, HEAD_PAD), lambda b, i: (b, i, 0)),
                  const((1, MLA_Q_RANK)), const((1, MLA_KV_RANK)),
                  const((MLA_Q_RANK, hw)), const((MLA_Q_RANK, hw)),
                  const((MLA_KV_RANK, hw)), const((MLA_KV_RANK, vw))],
        out_specs=[pl.BlockSpec((1, ts, hw), lambda b, i: (b, i, 0)),
                   pl.BlockSpec((1, ts, hw), lambda b, i: (b, i, 0)),
                   pl.BlockSpec((1, ts, vw), lambda b, i: (b, i, 0))],
        out_shape=[jax.ShapeDtypeStruct((bsz, seq, hw), BF16),
                   jax.ShapeDtypeStruct((bsz, seq, hw), BF16),
                   jax.ShapeDtypeStruct((bsz, seq, vw), BF16)],
        compiler_params=_params("parallel", "parallel"),
        name="mla_prep",
    )(proj3, proj3, proj3, proj3, cos, sin, q_norm[None, :], kv_norm[None, :], wq, wqr, wk, wv)


def _mla_attn_kernel(q_ref, k_ref, v_ref, o_ref, sa_ref, sb_ref, m_ref, acc_ref, *, t):
    qi = pl.program_id(2)
    row = lax.broadcasted_iota(jnp.int32, (t, t), 0)
    col = lax.broadcasted_iota(jnp.int32, (t, t), 1)
    allowed = (col // CHUNK) <= (row // CHUNK)
    ones = jnp.ones((t, LANES), BF16)
    m_ref[...] = jnp.full(m_ref.shape, -jnp.inf, F32)
    acc_ref[...] = jnp.zeros(acc_ref.shape, F32)

    def scores(kb, s_ref):
        kstart = pl.multiple_of(kb * t, t)
        for hh in range(2):
            lo = hh * HEAD_PAD
            s_ref[hh] = lax.dot_general(q_ref[0, :, lo:lo + HEAD_PAD],
                                        k_ref[0, pl.ds(kstart, t), lo:lo + HEAD_PAD], _NT,
                                        preferred_element_type=F32)

    def consume(kb, s_ref, masked):
        kstart = pl.multiple_of(kb * t, t)
        v_ext = jnp.concatenate([v_ref[0, pl.ds(kstart, t), :], ones], axis=1)
        for hh in range(2):
            s = s_ref[hh]
            if masked:
                s = jnp.where(allowed, s, -jnp.inf)
            m_prev = m_ref[hh]
            m_new = jnp.maximum(m_prev, jnp.max(s, axis=1, keepdims=True))
            alpha = jnp.exp2(m_prev - m_new)
            p = jnp.exp2(s - jnp.concatenate([m_new] * (t // LANES), axis=1))
            acc_ref[hh] = (jnp.concatenate([alpha, alpha], axis=1) * acc_ref[hh]
                           + jnp.dot(p.astype(BF16), v_ext, preferred_element_type=F32))
            m_ref[hh] = m_new

    scores(0, sa_ref)

    def body(j, carry):
        scores(2 * j + 1, sb_ref)
        consume(2 * j, sa_ref, False)
        scores(2 * j + 2, sa_ref)
        consume(2 * j + 1, sb_ref, False)
        return carry

    lax.fori_loop(0, lax.shift_right_logical(qi, 1), body, 0)

    @pl.when((qi & 1) == 0)
    def _():
        consume(qi, sa_ref, True)

    @pl.when((qi & 1) == 1)
    def _():
        scores(qi, sb_ref)
        consume(qi - 1, sa_ref, False)
        consume(qi, sb_ref, True)

    pv = 2 * MLA_V
    lane = lax.broadcasted_iota(jnp.int32, (t, pv), 1)
    out0 = acc_ref[0, :, :pv] / acc_ref[0, :, pv:]
    out1 = acc_ref[1, :, :pv] / acc_ref[1, :, pv:]
    o_ref[0] = jnp.where(lane < MLA_V, out0, out1).astype(BF16)


def _mla_attention(q, k, v):
    bsz, seq, _ = q.shape
    t = T_ATT
    pairs = MLA_HEADS // 2
    pv = 2 * MLA_V
    return pl.pallas_call(
        functools.partial(_mla_attn_kernel, t=t),
        grid=(bsz, pairs, seq // t),
        in_specs=[pl.BlockSpec((1, t, 2 * HEAD_PAD), lambda b, h, i: (b, i, h)),
                  pl.BlockSpec((1, seq, 2 * HEAD_PAD), lambda b, h, i: (b, 0, h)),
                  pl.BlockSpec((1, seq, pv), lambda b, h, i: (b, 0, h))],
        out_specs=pl.BlockSpec((1, t, pv), lambda b, h, i: (b, i, h)),
        out_shape=jax.ShapeDtypeStruct((bsz, seq, MLA_HEADS * MLA_V), BF16),
        scratch_shapes=[pltpu.VMEM((2, t, t), F32), pltpu.VMEM((2, t, t), F32),
                        pltpu.VMEM((2, t, LANES), F32),
                        pltpu.VMEM((2, t, pv + LANES), F32)],
        compiler_params=_params("parallel", "parallel", "arbitrary"),
        name="mla_attention",
    )(q, k, v)


def _sb_attn_kernel(q_ref, k_ref, v_ref, u_ref, o_ref, acc_ref, carry_ref, *, t, heads):
    qi = pl.program_id(2)
    row = lax.broadcasted_iota(jnp.int32, (t, t), 0)
    col = lax.broadcasted_iota(jnp.int32, (t, t), 1)
    before = col < row
    pw = 2 * SB_HEAD_DIM
    lane = lax.broadcasted_iota(jnp.int32, (t, pw), 1)
    reps = t // LANES
    acc_ref[...] = jnp.zeros(acc_ref.shape, F32)
    carry_ref[...] = jnp.zeros(carry_ref.shape, F32)

    def block(kstart, masked):
        for h in range(heads):
            lo = (h // 2) * pw
            q = q_ref[0, :, lo:lo + pw]
            in_head = (lane >= (h % 2) * SB_HEAD_DIM) & (lane < (h % 2 + 1) * SB_HEAD_DIM)
            qm = jnp.where(in_head, q, jnp.zeros_like(q))
            kblk = k_ref[0, pl.ds(kstart, t), lo:lo + pw]
            vblk = v_ref[0, pl.ds(kstart, t), lo:lo + pw]
            nz = lax.dot_general(qm, kblk, _NT, preferred_element_type=F32)
            log_stay = jnp.minimum(nz, 0.0) - jnp.log2(1.0 + jnp.exp2(-jnp.abs(nz)))
            if masked:
                log_stay = jnp.where(before, log_stay, 0.0)
            sums = jnp.dot(log_stay.astype(BF16), u_ref[...], preferred_element_type=F32)
            carry = carry_ref[h]
            own = log_stay - nz
            a = jnp.exp2(sums + (jnp.concatenate([carry] * reps, axis=1) + own))
            if masked:
                a = jnp.where(before, a, 0.0)
            acc_ref[h] += jnp.dot(a.astype(BF16), vblk, preferred_element_type=F32)
            carry_ref[h] = carry + jnp.sum(log_stay, axis=1, keepdims=True)

    block(pl.multiple_of(qi * t, t), True)

    def live():
        return jnp.max(carry_ref[...]) > SB_DEAD_LOG2

    def step(state):
        kb, _ = state
        block(pl.multiple_of(kb * t, t), False)
        return kb - 1, live()

    lax.while_loop(lambda st: (st[0] >= 0) & st[1], step, (qi - 1, live()))

    for pp in range(heads // 2):
        o_ref[0, :, pp * pw:(pp + 1) * pw] = jnp.where(
            lane < SB_HEAD_DIM, acc_ref[2 * pp], acc_ref[2 * pp + 1]).astype(BF16)


def _sb_attention(proj3):
    bsz, seq, _ = proj3.shape
    t = T_SB
    heads = SB_HEADS_PER_STEP
    w = heads * SB_HEAD_DIM
    r = lax.broadcasted_iota(jnp.int32, (t, t), 0)
    c = lax.broadcasted_iota(jnp.int32, (t, t), 1)
    suffix = (r > c).astype(BF16)
    return pl.pallas_call(
        functools.partial(_sb_attn_kernel, t=t, heads=heads),
        grid=(bsz, SB_HEADS // heads, seq // t),
        in_specs=[pl.BlockSpec((1, t, w), lambda b, h, i: (b, i, OFF_SBQ // w + h)),
                  pl.BlockSpec((1, seq, w), lambda b, h, i: (b, 0, OFF_SBK // w + h)),
                  pl.BlockSpec((1, seq, w), lambda b, h, i: (b, 0, OFF_SBV // w + h)),
                  pl.BlockSpec((t, t), lambda b, h, i: (0, 0))],
        out_specs=pl.BlockSpec((1, t, w), lambda b, h, i: (b, i, h)),
        out_shape=jax.ShapeDtypeStruct((bsz, seq, SB_HEADS * SB_HEAD_DIM), BF16),
        scratch_shapes=[pltpu.VMEM((heads, t, 2 * SB_HEAD_DIM), F32), pltpu.VMEM((heads, t, LANES), F32)],
        compiler_params=_params("parallel", "parallel", "arbitrary"),
        name="sb_attention",
    )(proj3, proj3, proj3, suffix)


def _merge_kernel(x_ref, b0_ref, b1_ref, b2_ref, b3_ref, wg_ref, bg_ref, wb_ref, wo_ref,
                  lg_ref, lb_ref, o_ref):
    rows = x_ref.shape[0] // MERGE_ROW_CHUNKS
    for c in range(MERGE_ROW_CHUNKS):
        sl = slice(c * rows, (c + 1) * rows)
        x = x_ref[sl, :]
        xb = x.astype(BF16)
        merged = None
        for n, br in enumerate((b0_ref, b1_ref, b2_ref, b3_ref)):
            c0 = n * D_MODEL
            pre = jnp.dot(xb, wg_ref[:, c0:c0 + D_MODEL], preferred_element_type=F32)
            gate = _sigmoid(pre + bg_ref[:, c0:c0 + D_MODEL])
            term = gate * jnp.dot(br[sl, :], wb_ref[n], preferred_element_type=F32)
            merged = term if merged is None else merged + term
        y = jnp.dot(merged.astype(BF16), wo_ref[...], preferred_element_type=F32)
        o_ref[sl, :] = _layer_norm_rows(DN_ALPHA * x + y, lg_ref[...], lb_ref[...])


def _merge(x2, branches, w_gate, b_gate, w_branch, w_out, ln_g, ln_b):
    n = x2.shape[0]
    tm = TM_MERGE
    nb = len(branches)

    def const2(shape):
        return pl.BlockSpec(shape, lambda i: (0, 0))

    return pl.pallas_call(
        _merge_kernel,
        grid=(n // tm,),
        in_specs=[pl.BlockSpec((tm, D_MODEL), lambda i: (i, 0))]
                 + [pl.BlockSpec((tm, BRANCH_WIDTH), lambda i: (i, 0))] * nb
                 + [const2((D_MODEL, nb * D_MODEL)), const2((1, nb * D_MODEL)),
                    pl.BlockSpec((nb, BRANCH_WIDTH, D_MODEL), lambda i: (0, 0, 0)),
                    const2((D_MODEL, D_MODEL)), const2((1, D_MODEL)), const2((1, D_MODEL))],
        out_specs=pl.BlockSpec((tm, D_MODEL), lambda i: (i, 0)),
        out_shape=jax.ShapeDtypeStruct((n, D_MODEL), F32),
        compiler_params=_params("parallel"),
        name="merge_out",
    )(x2, *branches, w_gate, b_gate[None, :], w_branch, w_out, ln_g[None, :], ln_b[None, :])


def _swiglu_rows(x_rows, wg, wu, wd):
    xb = x_rows.astype(BF16)
    g = jnp.dot(xb, wg, preferred_element_type=F32)
    u = jnp.dot(xb, wu, preferred_element_type=F32)
    h = (g * _sigmoid(g) * u).astype(BF16)
    return jnp.dot(h, wd, preferred_element_type=F32)


def _ffn_accumulate(x_ref, wg, wu, wd, acc_ref, j, finish):
    rows = x_ref.shape[0] // FFN_ROW_CHUNKS
    for c in range(FFN_ROW_CHUNKS):
        sl = slice(c * rows, (c + 1) * rows)
        x_rows = x_ref[sl, :]
        acc = jnp.where(j > 0, acc_ref[sl, :], 0.0) + _swiglu_rows(x_rows, wg, wu, wd)
        acc_ref[sl, :] = acc
        finish(sl, x_rows, acc)


def _dense_ffn_kernel(x_ref, wg_ref, wu_ref, wd_ref, lg_ref, lb_ref, o_ref):
    rows = x_ref.shape[0] // FFN_ROW_CHUNKS
    for c in range(FFN_ROW_CHUNKS):
        sl = slice(c * rows, (c + 1) * rows)
        x_rows = x_ref[sl, :]
        f = _swiglu_rows(x_rows, wg_ref[...], wu_ref[...], wd_ref[...])
        o_ref[sl, :] = _layer_norm_rows(DN_ALPHA * x_rows + f, lg_ref[...], lb_ref[...])


def _dense_ffn(x2, wg, wu, wd, ln_g, ln_b):
    n = x2.shape[0]
    tm = TM_FFN

    def resident(shape):
        return pl.BlockSpec(shape, lambda i: (0, 0), pipeline_mode=pl.Buffered(1))

    return pl.pallas_call(
        _dense_ffn_kernel,
        grid=(n // tm,),
        in_specs=[pl.BlockSpec((tm, D_MODEL), lambda i: (i, 0)),
                  resident((D_MODEL, D_FF)), resident((D_MODEL, D_FF)), resident((D_FF, D_MODEL)),
                  pl.BlockSpec((1, D_MODEL), lambda i: (0, 0)),
                  pl.BlockSpec((1, D_MODEL), lambda i: (0, 0))],
        out_specs=pl.BlockSpec((tm, D_MODEL), lambda i: (i, 0)),
        out_shape=jax.ShapeDtypeStruct((n, D_MODEL), F32),
        compiler_params=_params("parallel"),
        name="dense_ffn",
    )(x2, wg, wu, wd, ln_g[None, :], ln_b[None, :])


def _router_kernel(x_ref, wr_ref, tri_ref, meta_ref, cnt_ref, run_ref):
    @pl.when(pl.program_id(0) == 0)
    def _():
        run_ref[...] = jnp.zeros(run_ref.shape, F32)

    x = x_ref[...]
    w = wr_ref[...]
    x_hi = x.astype(BF16)
    x_lo = (x - x_hi.astype(F32)).astype(BF16)
    w_hi = w.astype(BF16)
    w_lo = (w - w_hi.astype(F32)).astype(BF16)
    logits = (jnp.dot(x_hi, w_hi, preferred_element_type=F32)
              + jnp.dot(x_lo, w_hi, preferred_element_type=F32)
              + jnp.dot(x_hi, w_lo, preferred_element_type=F32))
    lane = lax.broadcasted_iota(jnp.int32, logits.shape, 1)
    logits = jnp.where(lane < N_EXPERTS, logits, -jnp.inf)
    v1 = jnp.max(logits, axis=1, keepdims=True)
    i1 = jnp.min(jnp.where(logits == v1, lane, LANES), axis=1, keepdims=True)
    rest = jnp.where(lane == i1, -jnp.inf, logits)
    v2 = jnp.max(rest, axis=1, keepdims=True)
    i2 = jnp.min(jnp.where(rest == v2, lane, LANES), axis=1, keepdims=True)
    ex = jnp.exp(v2 - v1)
    den = 1.0 + ex
    w1 = 1.0 / den
    w2 = ex / den

    sel1 = lane == i1
    sel2 = lane == i2
    chosen = jnp.where(sel1 | sel2, 1.0, 0.0)
    run = run_ref[...]
    before = jnp.dot(tri_ref[...], chosen.astype(BF16), preferred_element_type=F32) + run
    r1 = jnp.sum(jnp.where(sel1, before, 0.0), axis=1, keepdims=True)
    r2 = jnp.sum(jnp.where(sel2, before, 0.0), axis=1, keepdims=True)
    run = run + jnp.sum(chosen, axis=0, keepdims=True)
    run_ref[...] = run
    cnt_ref[...] = run

    meta = jnp.where(lane == 0, i1.astype(F32), 0.0)
    meta = jnp.where(lane == 1, i2.astype(F32), meta)
    meta = jnp.where(lane == 2, w1, meta)
    meta = jnp.where(lane == 3, w2, meta)
    meta = jnp.where(lane == 4, r1, meta)
    meta = jnp.where(lane == 5, r2, meta)
    meta_ref[...] = meta


def _router(x2, w_router):
    n = x2.shape[0]
    tm = TM_ROUTE
    wr = jnp.pad(w_router, ((0, 0), (0, LANES - N_EXPERTS)))
    r = lax.broadcasted_iota(jnp.int32, (tm, tm), 0)
    c = lax.broadcasted_iota(jnp.int32, (tm, tm), 1)
    tri = (c < r).astype(BF16)
    return pl.pallas_call(
        _router_kernel,
        grid=(n // tm,),
        in_specs=[pl.BlockSpec((tm, D_MODEL), lambda i: (i, 0)),
                  pl.BlockSpec((D_MODEL, LANES), lambda i: (0, 0)),
                  pl.BlockSpec((tm, tm), lambda i: (0, 0))],
        out_specs=[pl.BlockSpec((tm, LANES), lambda i: (i, 0)),
                   pl.BlockSpec((1, LANES), lambda i: (0, 0))],
        out_shape=[jax.ShapeDtypeStruct((n, LANES), F32),
                   jax.ShapeDtypeStruct((1, LANES), F32)],
        scratch_shapes=[pltpu.VMEM((1, LANES), F32)],
        compiler_params=_params("arbitrary"),
        name="router",
    )(x2, wr, tri)


def _row_copy(src_hbm, dst_ref, src_row, dst_row, sem):
    return pltpu.make_async_copy(src_hbm.at[pl.ds(src_row, 1)], dst_ref.at[pl.ds(dst_row, 1)], sem)


def _all_rows_copy(src_hbm, dst_ref, sem):
    return pltpu.make_async_copy(src_hbm.at[pl.ds(0, dst_ref.shape[0])], dst_ref, sem)


def _dispatch_kernel(p1_ref, p2_ref, x_ref, init_hbm, o_hbm, sem, *, tt):
    del init_hbm

    def chunk(c, carry):
        for u in range(DMA_UNROLL):
            r = c * DMA_UNROLL + u
            src = x_ref.at[pl.ds(r, 1)]
            pltpu.make_async_copy(src, o_hbm.at[pl.ds(p1_ref[0, 0, r], 1)], sem.at[0]).start(priority=0)
            pltpu.make_async_copy(src, o_hbm.at[pl.ds(p2_ref[0, 0, r], 1)], sem.at[1]).start(priority=1)
        return carry

    lax.fori_loop(0, tt // DMA_UNROLL, chunk, 0)
    pltpu.make_async_copy(x_ref, o_hbm.at[pl.ds(0, tt)], sem.at[0]).wait()
    pltpu.make_async_copy(x_ref, o_hbm.at[pl.ds(0, tt)], sem.at[1]).wait()


def _dispatch_rows(x2, pos1, pos2, p):
    n = x2.shape[0]
    tt = TT_DISPATCH
    idx = pl.BlockSpec((1, 1, tt), lambda i: (i, 0, 0), memory_space=pltpu.SMEM)
    return pl.pallas_call(
        functools.partial(_dispatch_kernel, tt=tt),
        grid=(n // tt,),
        in_specs=[idx, idx,
                  pl.BlockSpec((tt, D_MODEL), lambda i: (i, 0)),
                  pl.BlockSpec(memory_space=pl.ANY)],
        out_specs=pl.BlockSpec(memory_space=pl.ANY),
        out_shape=jax.ShapeDtypeStruct((p, D_MODEL), x2.dtype),
        input_output_aliases={3: 0},
        scratch_shapes=[pltpu.SemaphoreType.DMA((2,))],
        compiler_params=_params("arbitrary"),
        name="moe_dispatch",
    )(pos1.reshape(n // tt, 1, tt), pos2.reshape(n // tt, 1, tt), x2, jnp.zeros((p, D_MODEL), x2.dtype))


def _moe_ffn_kernel(te_ref, nt_ref, x_ref, wg_ref, wu_ref, wd_ref, o_ref, acc_ref):
    i = pl.program_id(0)
    j = pl.program_id(1)

    @pl.when((i == 0) & (j == 0))
    def _():
        acc_ref[...] = jnp.zeros(acc_ref.shape, F32)

    @pl.when(i < nt_ref[0])
    def _():
        def finish(sl, x_rows, acc):
            o_ref[sl, :] = acc

        _ffn_accumulate(x_ref, wg_ref[0], wu_ref[0], wd_ref[0], acc_ref, j, finish)

    @pl.when(i >= nt_ref[0])
    def _():
        o_ref[...] = jnp.zeros(o_ref.shape, o_ref.dtype)


def _moe_ffn(xs, tile_expert, num_tiles, wg, wu, wd):
    p = xs.shape[0]
    tm, tf = TM_FFN, TF_FFN
    nj = D_FF // tf

    def row(i, j, te, nt):
        return (jnp.minimum(i, nt[0] - 1), 0)

    def col_j(i, j, nt):
        return jnp.where(i < nt[0], j, nj - 1)

    def w_up(i, j, te, nt):
        return (te[jnp.minimum(i, nt[0] - 1)], 0, col_j(i, j, nt))

    def w_down(i, j, te, nt):
        return (te[jnp.minimum(i, nt[0] - 1)], col_j(i, j, nt), 0)

    grid_spec = pltpu.PrefetchScalarGridSpec(
        num_scalar_prefetch=2,
        grid=(p // tm, nj),
        in_specs=[pl.BlockSpec((tm, D_MODEL), row),
                  pl.BlockSpec((1, D_MODEL, tf), w_up),
                  pl.BlockSpec((1, D_MODEL, tf), w_up),
                  pl.BlockSpec((1, tf, D_MODEL), w_down)],
        out_specs=pl.BlockSpec((tm, D_MODEL), lambda i, j, te, nt: (i, 0)),
        scratch_shapes=[pltpu.VMEM((tm, D_MODEL), F32)],
    )
    return pl.pallas_call(
        _moe_ffn_kernel,
        grid_spec=grid_spec,
        out_shape=jax.ShapeDtypeStruct((p, D_MODEL), F32),
        compiler_params=_params("arbitrary", "arbitrary"),
        name="moe_ffn",
    )(tile_expert, num_tiles, xs, wg, wu, wd)


def _combine_kernel(p1_ref, p2_ref, n1_ref, n2_ref, x_ref, w1_ref, w2_ref, y_hbm, lg_ref, lb_ref, o_ref,
                    buf1, buf2, sem, *, tc):
    i = pl.program_id(0)
    slot = i & 1

    def fetch(a_ref, b_ref, s):
        def chunk(c, carry):
            for u in range(DMA_UNROLL):
                r = c * DMA_UNROLL + u
                _row_copy(y_hbm, buf1.at[s], a_ref[0, 0, r], r, sem.at[0, s]).start(priority=0)
                _row_copy(y_hbm, buf2.at[s], b_ref[0, 0, r], r, sem.at[1, s]).start(priority=1)
            return carry

        lax.fori_loop(0, tc // DMA_UNROLL, chunk, 0)

    @pl.when(i == 0)
    def _():
        fetch(p1_ref, p2_ref, 0)

    @pl.when(i + 1 < pl.num_programs(0))
    def _():
        fetch(n1_ref, n2_ref, 1 - slot)

    _all_rows_copy(y_hbm, buf1.at[slot], sem.at[0, slot]).wait()
    _all_rows_copy(y_hbm, buf2.at[slot], sem.at[1, slot]).wait()
    f = w1_ref[...] * buf1[slot] + w2_ref[...] * buf2[slot]
    o_ref[...] = _layer_norm_rows(DN_ALPHA * x_ref[...] + f, lg_ref[...], lb_ref[...])


def _combine(x2, y, pos1, pos2, w1, w2, ln_g, ln_b):
    n = x2.shape[0]
    tc = TC_COMBINE
    steps = n // tc
    idx = pl.BlockSpec((1, 1, tc), lambda i: (i, 0, 0), memory_space=pltpu.SMEM)
    nxt = pl.BlockSpec((1, 1, tc), lambda i: (jnp.minimum(i + 1, steps - 1), 0, 0), memory_space=pltpu.SMEM)
    p1 = pos1.reshape(steps, 1, tc)
    p2 = pos2.reshape(steps, 1, tc)
    return pl.pallas_call(
        functools.partial(_combine_kernel, tc=tc),
        grid=(steps,),
        in_specs=[idx, idx, nxt, nxt,
                  pl.BlockSpec((tc, D_MODEL), lambda i: (i, 0)),
                  pl.BlockSpec((tc, 1), lambda i: (i, 0)),
                  pl.BlockSpec((tc, 1), lambda i: (i, 0)),
                  pl.BlockSpec(memory_space=pl.ANY),
                  pl.BlockSpec((1, D_MODEL), lambda i: (0, 0)),
                  pl.BlockSpec((1, D_MODEL), lambda i: (0, 0))],
        out_specs=pl.BlockSpec((tc, D_MODEL), lambda i: (i, 0)),
        out_shape=jax.ShapeDtypeStruct((n, D_MODEL), F32),
        scratch_shapes=[pltpu.VMEM((2, tc, D_MODEL), F32), pltpu.VMEM((2, tc, D_MODEL), F32),
                        pltpu.SemaphoreType.DMA((2, 2))],
        compiler_params=_params("arbitrary"),
        name="moe_combine",
    )(p1, p2, p1, p2, x2, w1, w2, y, ln_g[None, :], ln_b[None, :])


def _moe_layer(x2, w_router, wg, wu, wd, ln_g, ln_b):
    n = x2.shape[0]
    tm = TM_FFN
    n_tiles = (2 * n) // tm + N_EXPERTS
    p = n_tiles * tm

    meta, counts = _router(x2, w_router)
    e1 = meta[:, 0].astype(jnp.int32)
    e2 = meta[:, 1].astype(jnp.int32)
    w1 = meta[:, 2:3]
    w2 = meta[:, 3:4]
    r1 = meta[:, 4].astype(jnp.int32)
    r2 = meta[:, 5].astype(jnp.int32)
    cnt = counts[0, :N_EXPERTS].astype(jnp.int32)

    tiles_per = (cnt + tm - 1) // tm
    tile_end = jnp.cumsum(tiles_per)
    start = (tile_end - tiles_per) * tm
    pos1 = start[e1] + r1
    pos2 = start[e2] + r2
    num_tiles = tile_end[-1:]
    tile_expert = jnp.minimum(
        jnp.sum((jnp.arange(n_tiles)[:, None] >= tile_end[None, :]).astype(jnp.int32), axis=1),
        N_EXPERTS - 1)
    xs = _dispatch_rows(x2, pos1, pos2, p)
    y = _moe_ffn(xs, tile_expert, num_tiles, wg, wu, wd)
    return _combine(x2, y, pos1, pos2, w1, w2, ln_g, ln_b)


def _prep_in_proj(w):
    sc = w[:, 0:1536]
    cf = w[:, 1536:2560]
    cq = w[:, 2560:2816]
    ckv = w[:, 2816:2944]
    kr = w[:, 2944:2976]
    sb = w[:, 2976:4512]
    gate = w[:, 4512:]
    half = MLA_ROPE // 2
    z_nope = jnp.zeros((D_MODEL, MLA_NOPE), F32)
    z_tail = jnp.zeros((D_MODEL, HEAD_PAD - MLA_NOPE - MLA_ROPE), F32)
    kr_pad = jnp.concatenate([z_nope, kr, z_tail], axis=1)
    kr_rot = jnp.concatenate([z_nope, -kr[:, half:], kr[:, :half], z_tail], axis=1)
    hw = SB_HEADS * SB_HEAD_DIM
    sbq = sb[:, :hw] * (-(SB_HEAD_DIM ** -0.5) * math.log2(math.e))
    mix = jnp.concatenate([sc, cf, sbq, sb[:, hw:], cq, ckv, kr_pad, kr_rot], axis=1)
    return mix.astype(BF16), gate.astype(BF16)


def _prep_mla(w_uq, w_ukv):
    half = MLA_ROPE // 2
    wq = w_uq.reshape(MLA_Q_RANK, MLA_HEADS, MLA_NOPE + MLA_ROPE)
    nope, rope = wq[..., :MLA_NOPE], wq[..., MLA_NOPE:]
    tail = jnp.zeros((MLA_Q_RANK, MLA_HEADS, HEAD_PAD - MLA_NOPE - MLA_ROPE), F32)
    q_main = jnp.concatenate([nope, rope, tail], axis=-1)
    q_rot = jnp.concatenate([jnp.zeros_like(nope), -rope[..., half:], rope[..., :half], tail], axis=-1)
    wkv = w_ukv.reshape(MLA_KV_RANK, MLA_HEADS, MLA_NOPE + MLA_V)
    k_nope, v = wkv[..., :MLA_NOPE], wkv[..., MLA_NOPE:]
    k_main = jnp.concatenate(
        [k_nope, jnp.zeros((MLA_KV_RANK, MLA_HEADS, HEAD_PAD - MLA_NOPE), F32)], axis=-1)
    hw = MLA_HEADS * HEAD_PAD
    return (q_main.reshape(MLA_Q_RANK, hw).astype(BF16), q_rot.reshape(MLA_Q_RANK, hw).astype(BF16),
            k_main.reshape(MLA_KV_RANK, hw).astype(BF16),
            v.reshape(MLA_KV_RANK, MLA_HEADS * MLA_V).astype(BF16))


def kernel(x, positions, w_in, b_gate, sc_conv, cf_conv, cf_conv_bias, cf_ln_g, cf_ln_b,
           mla_q_norm, mla_w_uq, mla_kv_norm, mla_w_ukv, w_branch, w_out,
           ln_mix_g, ln_mix_b, ln_ffn_g, ln_ffn_b, ffn_w_gate, ffn_w_up, ffn_w_down,
           router_w, exp_w_gate, exp_w_up, exp_w_down):
    bsz, seq, d = x.shape
    n = bsz * seq
    cos, sin = _rope_tables(positions)
    x2 = x.reshape(n, d)
    for layer in range(DEPTH):
        w_mix, w_gate = _prep_in_proj(w_in[layer])
        wq, wqr, wk, wv = _prep_mla(mla_w_uq[layer], mla_w_ukv[layer])
        proj, br0, br1 = _inproj_conv(x2, w_mix, seq, sc_conv[layer], cf_conv[layer], cf_conv_bias[layer],
                                      cf_ln_g[layer], cf_ln_b[layer])
        proj = proj.reshape(bsz, seq, N_OUT)
        q, k, v = _mla_prep(proj, cos, sin, mla_q_norm[layer], mla_kv_norm[layer], wq, wqr, wk, wv)
        br2 = _mla_attention(q, k, v)
        br3 = _sb_attention(proj)
        branches = [b.reshape(n, BRANCH_WIDTH) for b in (br0, br1, br2, br3)]
        x2 = _merge(x2, branches, w_gate, b_gate[layer], w_branch[layer].astype(BF16),
                    w_out[layer].astype(BF16), ln_mix_g[layer], ln_mix_b[layer])
        i = layer // 2
        if layer % 2 == 0:
            x2 = _dense_ffn(x2, ffn_w_gate[i].astype(BF16), ffn_w_up[i].astype(BF16),
                            ffn_w_down[i].astype(BF16), ln_ffn_g[layer], ln_ffn_b[layer])
        else:
            x2 = _moe_layer(x2, router_w[i], exp_w_gate[i].astype(BF16), exp_w_up[i].astype(BF16),
                            exp_w_down[i].astype(BF16), ln_ffn_g[layer], ln_ffn_b[layer])
    return x2.reshape(bsz, seq, d)
```

```python
import functools
import math

import jax
import jax.numpy as jnp
from jax import lax
from jax.experimental import pallas as pl
from jax.experimental.pallas import tpu as pltpu

F32 = jnp.float32
BF16 = jnp.bfloat16

D_MODEL = 1024
DEPTH = 2
CHUNK = 64
BRANCH_WIDTH = 512
SC_CONV_LEN = 3
CF_CONV_LEN = 31
MLA_HEADS = 8
MLA_NOPE = 64
MLA_ROPE = 32
MLA_V = 64
MLA_Q_RANK = 256
MLA_KV_RANK = 128
ROPE_THETA = 10000.0
SB_HEADS = 8
SB_HEAD_DIM = 64
D_FF = 3584
N_EXPERTS = 8
DN_ALPHA = (2 * DEPTH) ** 0.25
LN_EPS = 1e-5
RMS_EPS = 1e-6
SB_DEAD_LOG2 = -150.0

LANES = 128
SUBLANES = 8
HEAD_PAD = 128
CONV_HALO = 32
VMEM_LIMIT = 56 * 1024 * 1024

OFF_SCB, OFF_SCC, OFF_SCH = 0, 512, 1024
OFF_CFV, OFF_CFG = 1536, 2048
N_CONV_IN = 2560
OFF_SBQ, OFF_SBK, OFF_SBV = 0, 512, 1024
OFF_CQ = 1536
OFF_CKV = 1792
OFF_KR = 1920
OFF_KRR = 2048
N_OUT = 2176
N_MIX = N_CONV_IN + N_OUT

TM_PROJ = 1024
PROJ_SUB_ROWS = 128
TS_PREP = 512
T_ATT = 512
T_SB = 256
MLA_HEADS_PER_STEP = 4
SB_HEADS_PER_STEP = 8
TM_MERGE = 512
TM_FFN = 512
TF_FFN = 1792
FFN_ROW_CHUNKS = 2
MERGE_ROW_CHUNKS = 2
TM_ROUTE = 512
TT_DISPATCH = 512
TC_COMBINE = 256
DMA_UNROLL = 8

_NT = (((1,), (1,)), ((), ()))


def _params(*sem):
    return pltpu.CompilerParams(dimension_semantics=sem, vmem_limit_bytes=VMEM_LIMIT)


def _sigmoid(x):
    return 1.0 / (1.0 + jnp.exp(-x))


def _layer_norm_rows(v, g, b):
    mu = jnp.mean(v, axis=-1, keepdims=True)
    c = v - mu
    var = jnp.mean(c * c, axis=-1, keepdims=True)
    return c * lax.rsqrt(var + LN_EPS) * g + b


def _rope_kernel(pos_ref, freq_ref, cos_ref, sin_ref):
    ang = pos_ref[0] * freq_ref[...]
    cos_ref[0] = jnp.cos(ang)
    sin_ref[0] = jnp.sin(ang)


def _rope_tables(positions):
    bsz, seq = positions.shape
    half = MLA_ROPE // 2
    inv_freq = 1.0 / (ROPE_THETA ** (jnp.arange(half, dtype=F32) * (2.0 / MLA_ROPE)))
    freq = jnp.concatenate([jnp.zeros((MLA_NOPE,), F32), inv_freq, inv_freq,
                            jnp.zeros((HEAD_PAD - MLA_NOPE - MLA_ROPE,), F32)])[None, :]
    pos = positions.astype(F32)[..., None]
    ts = TS_PREP
    return pl.pallas_call(
        _rope_kernel,
        grid=(bsz, seq // ts),
        in_specs=[pl.BlockSpec((1, ts, 1), lambda b, i: (b, i, 0)),
                  pl.BlockSpec((1, HEAD_PAD), lambda b, i: (0, 0))],
        out_specs=[pl.BlockSpec((1, ts, HEAD_PAD), lambda b, i: (b, i, 0))] * 2,
        out_shape=[jax.ShapeDtypeStruct((bsz, seq, HEAD_PAD), F32)] * 2,
        compiler_params=_params("parallel", "parallel"),
        name="rope_tables",
    )(pos, freq)


def _causal_taps(w_ref, buf, n_taps, base, rows):
    groups = {}
    for l in range(n_taps):
        off = CONV_HALO - (n_taps - 1) + l
        groups.setdefault(off % SUBLANES, []).append((l, off - off % SUBLANES))
    out = None
    for shift, taps in sorted(groups.items()):
        part = None
        for l, lo in taps:
            term = w_ref[l:l + 1, :] * buf[base + lo:base + lo + rows + SUBLANES, :]
            part = term if part is None else part + term
        part = part[shift:shift + rows, :]
        out = part if out is None else out + part
    return out


def _inproj_conv_kernel(x_ref, w_ref, wsc_ref, wcf_ref, bcf_ref, lng_ref, lnb_ref,
                        o_ref, o0_ref, o1_ref, bufc, bufu, *, tm, steps_per_seq):
    i = pl.program_id(0)
    halo = CONV_HALO
    w = BRANCH_WIDTH

    @pl.when(i == 0)
    def _():
        bufc[...] = jnp.zeros(bufc.shape, F32)
        bufu[...] = jnp.zeros(bufu.shape, F32)

    starts_sequence = (i % steps_per_seq) == 0
    bufc[0:halo, :] = jnp.where(starts_sequence, 0.0, bufc[tm:tm + halo, :])
    bufu[0:halo, :] = jnp.where(starts_sequence, 0.0, bufu[tm:tm + halo, :])

    def proj(xb, c0, width):
        return jnp.dot(xb, w_ref[:, c0:c0 + width], preferred_element_type=F32)

    rs = PROJ_SUB_ROWS
    for s0 in range(0, tm, rs):
        rows = slice(s0, s0 + rs)
        xb = x_ref[rows, :].astype(BF16)
        gate_b = proj(xb, OFF_SCB, w)
        bufc[halo + s0:halo + s0 + rs, :] = proj(xb, OFF_SCC, w) * proj(xb, OFF_SCH, w)
        bufu[halo + s0:halo + s0 + rs, :] = proj(xb, OFF_CFV, w) * _sigmoid(proj(xb, OFF_CFG, w))
        for c0 in range(0, N_OUT, 512):
            c1 = min(c0 + 512, N_OUT)
            o_ref[rows, c0:c1] = proj(xb, N_CONV_IN + c0, c1 - c0).astype(BF16)
        o0_ref[rows, :] = (gate_b * _causal_taps(wsc_ref, bufc, SC_CONV_LEN, s0, rs)).astype(BF16)
        y = _layer_norm_rows(_causal_taps(wcf_ref, bufu, CF_CONV_LEN, s0, rs) + bcf_ref[...],
                             lng_ref[...], lnb_ref[...])
        o1_ref[rows, :] = (y * _sigmoid(y)).astype(BF16)


def _inproj_conv(x2, w_mix, seq, sc_conv, cf_conv, cf_bias, cf_ln_g, cf_ln_b):
    n = x2.shape[0]
    tm = TM_PROJ
    w = BRANCH_WIDTH

    def const(shape):
        return pl.BlockSpec(shape, lambda i: (0, 0))

    return pl.pallas_call(
        functools.partial(_inproj_conv_kernel, tm=tm, steps_per_seq=seq // tm),
        grid=(n // tm,),
        in_specs=[pl.BlockSpec((tm, D_MODEL), lambda i: (i, 0)),
                  const((D_MODEL, N_MIX)),
                  const((SC_CONV_LEN, w)), const((CF_CONV_LEN, w)), const((1, w)), const((1, w)), const((1, w))],
        out_specs=[pl.BlockSpec((tm, N_OUT), lambda i: (i, 0)),
                   pl.BlockSpec((tm, w), lambda i: (i, 0)),
                   pl.BlockSpec((tm, w), lambda i: (i, 0))],
        out_shape=[jax.ShapeDtypeStruct((n, N_OUT), BF16),
                   jax.ShapeDtypeStruct((n, w), BF16),
                   jax.ShapeDtypeStruct((n, w), BF16)],
        scratch_shapes=[pltpu.VMEM((CONV_HALO + tm + SUBLANES, w), F32),
                        pltpu.VMEM((CONV_HALO + tm + SUBLANES, w), F32)],
        compiler_params=_params("arbitrary"),
        name="in_proj_conv",
    )(x2, w_mix, sc_conv, cf_conv, cf_bias[None, :], cf_ln_g[None, :], cf_ln_b[None, :])


def _mla_prep_kernel(cq_ref, ckv_ref, kr_ref, krr_ref, cos_ref, sin_ref, qn_ref, kvn_ref,
                     wq_ref, wqr_ref, wk_ref, wv_ref, q_ref, k_ref, v_ref):
    cos = cos_ref[0]
    sin = sin_ref[0]
    cos8 = jnp.concatenate([cos] * MLA_HEADS, axis=1)
    sin8 = jnp.concatenate([sin] * MLA_HEADS, axis=1)
    scale = (MLA_NOPE + MLA_ROPE) ** -0.5 * math.log2(math.e)

    cq = cq_ref[0].astype(F32)
    nq = cq * lax.rsqrt(jnp.mean(cq * cq, axis=-1, keepdims=True) + RMS_EPS) * qn_ref[...]
    nq = nq.astype(BF16)
    q1 = jnp.dot(nq, wq_ref[...], preferred_element_type=F32)
    q2 = jnp.dot(nq, wqr_ref[...], preferred_element_type=F32)
    q_ref[0] = ((q1 * cos8 + q2 * sin8) * scale).astype(BF16)

    ckv = ckv_ref[0].astype(F32)
    nkv = ckv * lax.rsqrt(jnp.mean(ckv * ckv, axis=-1, keepdims=True) + RMS_EPS) * kvn_ref[...]
    nkv = nkv.astype(BF16)
    k_rope = kr_ref[0].astype(F32) * cos + krr_ref[0].astype(F32) * sin
    k_nope = jnp.dot(nkv, wk_ref[...], preferred_element_type=F32)
    k_ref[0] = (k_nope + jnp.concatenate([k_rope] * MLA_HEADS, axis=1)).astype(BF16)
    v_ref[0] = jnp.dot(nkv, wv_ref[...], preferred_element_type=F32).astype(BF16)


def _mla_prep(proj3, cos, sin, q_norm, kv_norm, wq, wqr, wk, wv):
    bsz, seq, _ = proj3.shape
    ts = TS_PREP
    hw = MLA_HEADS * HEAD_PAD
    vw = MLA_HEADS * MLA_V

    def const(shape):
        return pl.BlockSpec(shape, lambda b, i: (0, 0))

    return pl.pallas_call(
        _mla_prep_kernel,
        grid=(bsz, seq // ts),
        in_specs=[pl.BlockSpec((1, ts, MLA_Q_RANK), lambda b, i: (b, i, OFF_CQ // MLA_Q_RANK)),
                  pl.BlockSpec((1, ts, MLA_KV_RANK), lambda b, i: (b, i, OFF_CKV // MLA_KV_RANK)),
                  pl.BlockSpec((1, ts, HEAD_PAD), lambda b, i: (b, i, OFF_KR // HEAD_PAD)),
                  pl.BlockSpec((1, ts, HEAD_PAD), lambda b, i: (b, i, OFF_KRR // HEAD_PAD)),
                  pl.BlockSpec((1, ts, HEAD_PAD), lambda b, i: (b, i, 0)),
                  pl.BlockSpec((1, ts, HEAD_PAD), lambda b, i: (b, i, 0)),
                  const((1, MLA_Q_RANK)), const((1, MLA_KV_RANK)),
                  const((MLA_Q_RANK, hw)), const((MLA_Q_RANK, hw)),
                  const((MLA_KV_RANK, hw)), const((MLA_KV_RANK, vw))],
        out_specs=[pl.BlockSpec((1, ts, hw), lambda b, i: (b, i, 0)),
                   pl.BlockSpec((1, ts, hw), lambda b, i: (b, i, 0)),
                   pl.BlockSpec((1, ts, vw), lambda b, i: (b, i, 0))],
        out_shape=[jax.ShapeDtypeStruct((bsz, seq, hw), BF16),
                   jax.ShapeDtypeStruct((bsz, seq, hw), BF16),
                   jax.ShapeDtypeStruct((bsz, seq, vw), BF16)],
        compiler_params=_params("parallel", "parallel"),
        name="mla_prep",
    )(proj3, proj3, proj3, proj3, cos, sin, q_norm[None, :], kv_norm[None, :], wq, wqr, wk, wv)


def _mla_attn_kernel(q_ref, k_ref, v_ref, o_ref, sa_ref, sb_ref, m_ref, acc_ref, *, t, heads):
    qi = pl.program_id(2)
    row = lax.broadcasted_iota(jnp.int32, (t, t), 0)
    col = lax.broadcasted_iota(jnp.int32, (t, t), 1)
    allowed = (col // CHUNK) <= (row // CHUNK)
    ones = jnp.ones((t, LANES), BF16)
    m_ref[...] = jnp.full(m_ref.shape, -jnp.inf, F32)
    acc_ref[...] = jnp.zeros(acc_ref.shape, F32)

    def scores(kb, s_ref):
        kstart = pl.multiple_of(kb * t, t)
        for hh in range(heads):
            lo = hh * HEAD_PAD
            s_ref[hh] = lax.dot_general(q_ref[0, :, lo:lo + HEAD_PAD],
                                        k_ref[0, pl.ds(kstart, t), lo:lo + HEAD_PAD], _NT,
                                        preferred_element_type=F32)

    def consume(kb, s_ref, masked):
        kstart = pl.multiple_of(kb * t, t)
        for hh in range(heads):
            vlo = (hh // 2) * 2 * MLA_V
            v_ext = jnp.concatenate([v_ref[0, pl.ds(kstart, t), vlo:vlo + 2 * MLA_V], ones], axis=1)
            s = s_ref[hh]
            if masked:
                s = jnp.where(allowed, s, -jnp.inf)
            m_prev = m_ref[hh]
            m_new = jnp.maximum(m_prev, jnp.max(s, axis=1, keepdims=True))
            alpha = jnp.exp2(m_prev - m_new)
            p = jnp.exp2(s - jnp.concatenate([m_new] * (t // LANES), axis=1))
            acc_ref[hh] = (jnp.concatenate([alpha, alpha], axis=1) * acc_ref[hh]
                           + jnp.dot(p.astype(BF16), v_ext, preferred_element_type=F32))
            m_ref[hh] = m_new

    scores(0, sa_ref)

    def body(j, carry):
        scores(2 * j + 1, sb_ref)
        consume(2 * j, sa_ref, False)
        scores(2 * j + 2, sa_ref)
        consume(2 * j + 1, sb_ref, False)
        return carry

    lax.fori_loop(0, lax.shift_right_logical(qi, 1), body, 0)

    @pl.when((qi & 1) == 0)
    def _():
        consume(qi, sa_ref, True)

    @pl.when((qi & 1) == 1)
    def _():
        scores(qi, sb_ref)
        consume(qi - 1, sa_ref, False)
        consume(qi, sb_ref, True)

    pv = 2 * MLA_V
    lane = lax.broadcasted_iota(jnp.int32, (t, pv), 1)
    for pp in range(heads // 2):
        out0 = acc_ref[2 * pp, :, :pv] / acc_ref[2 * pp, :, pv:]
        out1 = acc_ref[2 * pp + 1, :, :pv] / acc_ref[2 * pp + 1, :, pv:]
        o_ref[0, :, pp * pv:(pp + 1) * pv] = jnp.where(lane < MLA_V, out0, out1).astype(BF16)


def _mla_attention(q, k, v):
    bsz, seq, _ = q.shape
    t = T_ATT
    heads = MLA_HEADS_PER_STEP
    pv = 2 * MLA_V
    qw = heads * HEAD_PAD
    vw = heads * MLA_V
    return pl.pallas_call(
        functools.partial(_mla_attn_kernel, t=t, heads=heads),
        grid=(bsz, MLA_HEADS // heads, seq // t),
        in_specs=[pl.BlockSpec((1, t, qw), lambda b, h, i: (b, i, h)),
                  pl.BlockSpec((1, seq, qw), lambda b, h, i: (b, 0, h)),
                  pl.BlockSpec((1, seq, vw), lambda b, h, i: (b, 0, h))],
        out_specs=pl.BlockSpec((1, t, vw), lambda b, h, i: (b, i, h)),
        out_shape=jax.ShapeDtypeStruct((bsz, seq, MLA_HEADS * MLA_V), BF16),
        scratch_shapes=[pltpu.VMEM((heads, t, t), F32), pltpu.VMEM((heads, t, t), F32),
                        pltpu.VMEM((heads, t, LANES), F32),
                        pltpu.VMEM((heads, t, pv + LANES), F32)],
        compiler_params=_params("parallel", "parallel", "arbitrary"),
        name="mla_attention",
    )(q, k, v)


def _sb_attn_kernel(q_ref, k_ref, v_ref, u_ref, o_ref, acc_ref, carry_ref, *, t, heads):
    qi = pl.program_id(2)
    row = lax.broadcasted_iota(jnp.int32, (t, t), 0)
    col = lax.broadcasted_iota(jnp.int32, (t, t), 1)
    before = col < row
    pw = 2 * SB_HEAD_DIM
    lane = lax.broadcasted_iota(jnp.int32, (t, pw), 1)
    reps = t // LANES
    acc_ref[...] = jnp.zeros(acc_ref.shape, F32)
    carry_ref[...] = jnp.zeros(carry_ref.shape, F32)

    def block(kstart, masked):
        for h in range(heads):
            lo = (h // 2) * pw
            q = q_ref[0, :, lo:lo + pw]
            in_head = (lane >= (h % 2) * SB_HEAD_DIM) & (lane < (h % 2 + 1) * SB_HEAD_DIM)
            qm = jnp.where(in_head, q, jnp.zeros_like(q))
            kblk = k_ref[0, pl.ds(kstart, t), lo:lo + pw]
            vblk = v_ref[0, pl.ds(kstart, t), lo:lo + pw]
            nz = lax.dot_general(qm, kblk, _NT, preferred_element_type=F32)
            log_stay = jnp.minimum(nz, 0.0) - jnp.log2(1.0 + jnp.exp2(-jnp.abs(nz)))
            if masked:
                log_stay = jnp.where(before, log_stay, 0.0)
            sums = jnp.dot(log_stay.astype(BF16), u_ref[...], preferred_element_type=F32)
            carry = carry_ref[h]
            own = log_stay - nz
            a = jnp.exp2(sums + (jnp.concatenate([carry] * reps, axis=1) + own))
            if masked:
                a = jnp.where(before, a, 0.0)
            acc_ref[h] += jnp.dot(a.astype(BF16), vblk, preferred_element_type=F32)
            carry_ref[h] = carry + jnp.sum(log_stay, axis=1, keepdims=True)

    block(pl.multiple_of(qi * t, t), True)

    def live():
        return jnp.max(carry_ref[...]) > SB_DEAD_LOG2

    def step(state):
        kb, _ = state
        block(pl.multiple_of(kb * t, t), False)
        return kb - 1, live()

    lax.while_loop(lambda st: (st[0] >= 0) & st[1], step, (qi - 1, live()))

    for pp in range(heads // 2):
        o_ref[0, :, pp * pw:(pp + 1) * pw] = jnp.where(
            lane < SB_HEAD_DIM, acc_ref[2 * pp], acc_ref[2 * pp + 1]).astype(BF16)


def _sb_attention(proj3):
    bsz, seq, _ = proj3.shape
    t = T_SB
    heads = SB_HEADS_PER_STEP
    w = heads * SB_HEAD_DIM
    r = lax.broadcasted_iota(jnp.int32, (t, t), 0)
    c = lax.broadcasted_iota(jnp.int32, (t, t), 1)
    suffix = (r > c).astype(BF16)
    return pl.pallas_call(
        functools.partial(_sb_attn_kernel, t=t, heads=heads),
        grid=(bsz, SB_HEADS // heads, seq // t),
        in_specs=[pl.BlockSpec((1, t, w), lambda b, h, i: (b, i, OFF_SBQ // w + h)),
                  pl.BlockSpec((1, seq, w), lambda b, h, i: (b, 0, OFF_SBK // w + h)),
                  pl.BlockSpec((1, seq, w), lambda b, h, i: (b, 0, OFF_SBV // w + h)),
                  pl.BlockSpec((t, t), lambda b, h, i: (0, 0))],
        out_specs=pl.BlockSpec((1, t, w), lambda b, h, i: (b, i, h)),
        out_shape=jax.ShapeDtypeStruct((bsz, seq, SB_HEADS * SB_HEAD_DIM), BF16),
        scratch_shapes=[pltpu.VMEM((heads, t, 2 * SB_HEAD_DIM), F32), pltpu.VMEM((heads, t, LANES), F32)],
        compiler_params=_params("parallel", "parallel", "arbitrary"),
        name="sb_attention",
    )(proj3, proj3, proj3, suffix)


def _merge_kernel(x_ref, b0_ref, b1_ref, b2_ref, b3_ref, wg_ref, bg_ref, wb_ref, wo_ref,
                  lg_ref, lb_ref, o_ref):
    rows = x_ref.shape[0] // MERGE_ROW_CHUNKS
    for c in range(MERGE_ROW_CHUNKS):
        sl = slice(c * rows, (c + 1) * rows)
        x = x_ref[sl, :]
        xb = x.astype(BF16)
        merged = None
        for n, br in enumerate((b0_ref, b1_ref, b2_ref, b3_ref)):
            c0 = n * D_MODEL
            pre = jnp.dot(xb, wg_ref[:, c0:c0 + D_MODEL], preferred_element_type=F32)
            gate = _sigmoid(pre + bg_ref[:, c0:c0 + D_MODEL])
            term = gate * jnp.dot(br[sl, :], wb_ref[n], preferred_element_type=F32)
            merged = term if merged is None else merged + term
        y = jnp.dot(merged.astype(BF16), wo_ref[...], preferred_element_type=F32)
        o_ref[sl, :] = _layer_norm_rows(DN_ALPHA * x + y, lg_ref[...], lb_ref[...])


def _merge(x2, branches, w_gate, b_gate, w_branch, w_out, ln_g, ln_b):
    n = x2.shape[0]
    tm = TM_MERGE
    nb = len(branches)

    def const2(shape):
        return pl.BlockSpec(shape, lambda i: (0, 0))

    return pl.pallas_call(
        _merge_kernel,
        grid=(n // tm,),
        in_specs=[pl.BlockSpec((tm, D_MODEL), lambda i: (i, 0))]
                 + [pl.BlockSpec((tm, BRANCH_WIDTH), lambda i: (i, 0))] * nb
                 + [const2((D_MODEL, nb * D_MODEL)), const2((1, nb * D_MODEL)),
                    pl.BlockSpec((nb, BRANCH_WIDTH, D_MODEL), lambda i: (0, 0, 0)),
                    const2((D_MODEL, D_MODEL)), const2((1, D_MODEL)), const2((1, D_MODEL))],
        out_specs=pl.BlockSpec((tm, D_MODEL), lambda i: (i, 0)),
        out_shape=jax.ShapeDtypeStruct((n, D_MODEL), F32),
        compiler_params=_params("parallel"),
        name="merge_out",
    )(x2, *branches, w_gate, b_gate[None, :], w_branch, w_out, ln_g[None, :], ln_b[None, :])


def _swiglu_rows(x_rows, wg, wu, wd):
    xb = x_rows.astype(BF16)
    g = jnp.dot(xb, wg, preferred_element_type=F32)
    u = jnp.dot(xb, wu, preferred_element_type=F32)
    h = (g * _sigmoid(g) * u).astype(BF16)
    return jnp.dot(h, wd, preferred_element_type=F32)


def _ffn_accumulate(x_ref, wg, wu, wd, acc_ref, j, finish):
    rows = x_ref.shape[0] // FFN_ROW_CHUNKS
    for c in range(FFN_ROW_CHUNKS):
        sl = slice(c * rows, (c + 1) * rows)
        x_rows = x_ref[sl, :]
        acc = jnp.where(j > 0, acc_ref[sl, :], 0.0) + _swiglu_rows(x_rows, wg, wu, wd)
        acc_ref[sl, :] = acc
        finish(sl, x_rows, acc)


def _dense_ffn_kernel(x_ref, wg_ref, wu_ref, wd_ref, lg_ref, lb_ref, o_ref):
    rows = x_ref.shape[0] // FFN_ROW_CHUNKS
    for c in range(FFN_ROW_CHUNKS):
        sl = slice(c * rows, (c + 1) * rows)
        x_rows = x_ref[sl, :]
        f = _swiglu_rows(x_rows, wg_ref[...], wu_ref[...], wd_ref[...])
        o_ref[sl, :] = _layer_norm_rows(DN_ALPHA * x_rows + f, lg_ref[...], lb_ref[...])


def _dense_ffn(x2, wg, wu, wd, ln_g, ln_b):
    n = x2.shape[0]
    tm = TM_FFN

    def resident(shape):
        return pl.BlockSpec(shape, lambda i: (0, 0), pipeline_mode=pl.Buffered(1))

    return pl.pallas_call(
        _dense_ffn_kernel,
        grid=(n // tm,),
        in_specs=[pl.BlockSpec((tm, D_MODEL), lambda i: (i, 0)),
                  resident((D_MODEL, D_FF)), resident((D_MODEL, D_FF)), resident((D_FF, D_MODEL)),
                  pl.BlockSpec((1, D_MODEL), lambda i: (0, 0)),
                  pl.BlockSpec((1, D_MODEL), lambda i: (0, 0))],
        out_specs=pl.BlockSpec((tm, D_MODEL), lambda i: (i, 0)),
        out_shape=jax.ShapeDtypeStruct((n, D_MODEL), F32),
        compiler_params=_params("parallel"),
        name="dense_ffn",
    )(x2, wg, wu, wd, ln_g[None, :], ln_b[None, :])


def _router_kernel(x_ref, wr_ref, tri_ref, meta_ref, cnt_ref, run_ref):
    @pl.when(pl.program_id(0) == 0)
    def _():
        run_ref[...] = jnp.zeros(run_ref.shape, F32)

    x = x_ref[...]
    w = wr_ref[...]
    x_hi = x.astype(BF16)
    x_lo = (x - x_hi.astype(F32)).astype(BF16)
    w_hi = w.astype(BF16)
    w_lo = (w - w_hi.astype(F32)).astype(BF16)
    logits = (jnp.dot(x_hi, w_hi, preferred_element_type=F32)
              + jnp.dot(x_lo, w_hi, preferred_element_type=F32)
              + jnp.dot(x_hi, w_lo, preferred_element_type=F32))
    lane = lax.broadcasted_iota(jnp.int32, logits.shape, 1)
    logits = jnp.where(lane < N_EXPERTS, logits, -jnp.inf)
    v1 = jnp.max(logits, axis=1, keepdims=True)
    i1 = jnp.min(jnp.where(logits == v1, lane, LANES), axis=1, keepdims=True)
    rest = jnp.where(lane == i1, -jnp.inf, logits)
    v2 = jnp.max(rest, axis=1, keepdims=True)
    i2 = jnp.min(jnp.where(rest == v2, lane, LANES), axis=1, keepdims=True)
    ex = jnp.exp(v2 - v1)
    den = 1.0 + ex
    w1 = 1.0 / den
    w2 = ex / den

    sel1 = lane == i1
    sel2 = lane == i2
    chosen = jnp.where(sel1 | sel2, 1.0, 0.0)
    run = run_ref[...]
    before = jnp.dot(tri_ref[...], chosen.astype(BF16), preferred_element_type=F32) + run
    r1 = jnp.sum(jnp.where(sel1, before, 0.0), axis=1, keepdims=True)
    r2 = jnp.sum(jnp.where(sel2, before, 0.0), axis=1, keepdims=True)
    run = run + jnp.sum(chosen, axis=0, keepdims=True)
    run_ref[...] = run
    cnt_ref[...] = run

    meta = jnp.where(lane == 0, i1.astype(F32), 0.0)
    meta = jnp.where(lane == 1, i2.astype(F32), meta)
    meta = jnp.where(lane == 2, w1, meta)
    meta = jnp.where(lane == 3, w2, meta)
    meta = jnp.where(lane == 4, r1, meta)
    meta = jnp.where(lane == 5, r2, meta)
    meta_ref[...] = meta


def _router(x2, w_router):
    n = x2.shape[0]
    tm = TM_ROUTE
    wr = jnp.pad(w_router, ((0, 0), (0, LANES - N_EXPERTS)))
    r = lax.broadcasted_iota(jnp.int32, (tm, tm), 0)
    c = lax.broadcasted_iota(jnp.int32, (tm, tm), 1)
    tri = (c < r).astype(BF16)
    return pl.pallas_call(
        _router_kernel,
        grid=(n // tm,),
        in_specs=[pl.BlockSpec((tm, D_MODEL), lambda i: (i, 0)),
                  pl.BlockSpec((D_MODEL, LANES), lambda i: (0, 0)),
                  pl.BlockSpec((tm, tm), lambda i: (0, 0))],
        out_specs=[pl.BlockSpec((tm, LANES), lambda i: (i, 0)),
                   pl.BlockSpec((1, LANES), lambda i: (0, 0))],
        out_shape=[jax.ShapeDtypeStruct((n, LANES), F32),
                   jax.ShapeDtypeStruct((1, LANES), F32)],
        scratch_shapes=[pltpu.VMEM((1, LANES), F32)],
        compiler_params=_params("arbitrary"),
        name="router",
    )(x2, wr, tri)


def _row_copy(src_hbm, dst_ref, src_row, dst_row, sem):
    return pltpu.make_async_copy(src_hbm.at[pl.ds(src_row, 1)], dst_ref.at[pl.ds(dst_row, 1)], sem)


def _all_rows_copy(src_hbm, dst_ref, sem):
    return pltpu.make_async_copy(src_hbm.at[pl.ds(0, dst_ref.shape[0])], dst_ref, sem)


def _dispatch_kernel(p1_ref, p2_ref, x_ref, init_hbm, o_hbm, sem, *, tt):
    del init_hbm

    def chunk(c, carry):
        for u in range(DMA_UNROLL):
            r = c * DMA_UNROLL + u
            src = x_ref.at[pl.ds(r, 1)]
            pltpu.make_async_copy(src, o_hbm.at[pl.ds(p1_ref[0, 0, r], 1)], sem.at[0]).start(priority=0)
            pltpu.make_async_copy(src, o_hbm.at[pl.ds(p2_ref[0, 0, r], 1)], sem.at[1]).start(priority=1)
        return carry

    lax.fori_loop(0, tt // DMA_UNROLL, chunk, 0)
    pltpu.make_async_copy(x_ref, o_hbm.at[pl.ds(0, tt)], sem.at[0]).wait()
    pltpu.make_async_copy(x_ref, o_hbm.at[pl.ds(0, tt)], sem.at[1]).wait()


def _dispatch_rows(x2, pos1, pos2, p):
    n = x2.shape[0]
    tt = TT_DISPATCH
    idx = pl.BlockSpec((1, 1, tt), lambda i: (i, 0, 0), memory_space=pltpu.SMEM)
    return pl.pallas_call(
        functools.partial(_dispatch_kernel, tt=tt),
        grid=(n // tt,),
        in_specs=[idx, idx,
                  pl.BlockSpec((tt, D_MODEL), lambda i: (i, 0)),
                  pl.BlockSpec(memory_space=pl.ANY)],
        out_specs=pl.BlockSpec(memory_space=pl.ANY),
        out_shape=jax.ShapeDtypeStruct((p, D_MODEL), x2.dtype),
        input_output_aliases={3: 0},
        scratch_shapes=[pltpu.SemaphoreType.DMA((2,))],
        compiler_params=_params("arbitrary"),
        name="moe_dispatch",
    )(pos1.reshape(n // tt, 1, tt), pos2.reshape(n // tt, 1, tt), x2, jnp.zeros((p, D_MODEL), x2.dtype))


def _moe_ffn_kernel(te_ref, nt_ref, x_ref, wg_ref, wu_ref, wd_ref, o_ref, acc_ref):
    i = pl.program_id(0)
    j = pl.program_id(1)

    @pl.when((i == 0) & (j == 0))
    def _():
        acc_ref[...] = jnp.zeros(acc_ref.shape, F32)

    @pl.when(i < nt_ref[0])
    def _():
        def finish(sl, x_rows, acc):
            o_ref[sl, :] = acc

        _ffn_accumulate(x_ref, wg_ref[0], wu_ref[0], wd_ref[0], acc_ref, j, finish)

    @pl.when(i >= nt_ref[0])
    def _():
        o_ref[...] = jnp.zeros(o_ref.shape, o_ref.dtype)


def _moe_ffn(xs, tile_expert, num_tiles, wg, wu, wd):
    p = xs.shape[0]
    tm, tf = TM_FFN, TF_FFN
    nj = D_FF // tf

    def row(i, j, te, nt):
        return (jnp.minimum(i, nt[0] - 1), 0)

    def col_j(i, j, nt):
        return jnp.where(i < nt[0], j, nj - 1)

    def w_up(i, j, te, nt):
        return (te[jnp.minimum(i, nt[0] - 1)], 0, col_j(i, j, nt))

    def w_down(i, j, te, nt):
        return (te[jnp.minimum(i, nt[0] - 1)], col_j(i, j, nt), 0)

    grid_spec = pltpu.PrefetchScalarGridSpec(
        num_scalar_prefetch=2,
        grid=(p // tm, nj),
        in_specs=[pl.BlockSpec((tm, D_MODEL), row),
                  pl.BlockSpec((1, D_MODEL, tf), w_up),
                  pl.BlockSpec((1, D_MODEL, tf), w_up),
                  pl.BlockSpec((1, tf, D_MODEL), w_down)],
        out_specs=pl.BlockSpec((tm, D_MODEL), lambda i, j, te, nt: (i, 0)),
        scratch_shapes=[pltpu.VMEM((tm, D_MODEL), F32)],
    )
    return pl.pallas_call(
        _moe_ffn_kernel,
        grid_spec=grid_spec,
        out_shape=jax.ShapeDtypeStruct((p, D_MODEL), F32),
        compiler_params=_params("arbitrary", "arbitrary"),
        name="moe_ffn",
    )(tile_expert, num_tiles, xs, wg, wu, wd)


def _combine_kernel(p1_ref, p2_ref, n1_ref, n2_ref, x_ref, w1_ref, w2_ref, y_hbm, lg_ref, lb_ref, o_ref,
                    buf1, buf2, sem, *, tc):
    i = pl.program_id(0)
    slot = i & 1

    def fetch(a_ref, b_ref, s):
        def chunk(c, carry):
            for u in range(DMA_UNROLL):
                r = c * DMA_UNROLL + u
                _row_copy(y_hbm, buf1.at[s], a_ref[0, 0, r], r, sem.at[0, s]).start(priority=0)
                _row_copy(y_hbm, buf2.at[s], b_ref[0, 0, r], r, sem.at[1, s]).start(priority=1)
            return carry

        lax.fori_loop(0, tc // DMA_UNROLL, chunk, 0)

    @pl.when(i == 0)
    def _():
        fetch(p1_ref, p2_ref, 0)

    @pl.when(i + 1 < pl.num_programs(0))
    def _():
        fetch(n1_ref, n2_ref, 1 - slot)

    _all_rows_copy(y_hbm, buf1.at[slot], sem.at[0, slot]).wait()
    _all_rows_copy(y_hbm, buf2.at[slot], sem.at[1, slot]).wait()
    f = w1_ref[...] * buf1[slot] + w2_ref[...] * buf2[slot]
    o_ref[...] = _layer_norm_rows(DN_ALPHA * x_ref[...] + f, lg_ref[...], lb_ref[...])


def _combine(x2, y, pos1, pos2, w1, w2, ln_g, ln_b):
    n = x2.shape[0]
    tc = TC_COMBINE
    steps = n // tc
    idx = pl.BlockSpec((1, 1, tc), lambda i: (i, 0, 0), memory_space=pltpu.SMEM)
    nxt = pl.BlockSpec((1, 1, tc), lambda i: (jnp.minimum(i + 1, steps - 1), 0, 0), memory_space=pltpu.SMEM)
    p1 = pos1.reshape(steps, 1, tc)
    p2 = pos2.reshape(steps, 1, tc)
    return pl.pallas_call(
        functools.partial(_combine_kernel, tc=tc),
        grid=(steps,),
        in_specs=[idx, idx, nxt, nxt,
                  pl.BlockSpec((tc, D_MODEL), lambda i: (i, 0)),
                  pl.BlockSpec((tc, 1), lambda i: (i, 0)),
                  pl.BlockSpec((tc, 1), lambda i: (i, 0)),
                  pl.BlockSpec(memory_space=pl.ANY),
                  pl.BlockSpec((1, D_MODEL), lambda i: (0, 0)),
                  pl.BlockSpec((1, D_MODEL), lambda i: (0, 0))],
        out_specs=pl.BlockSpec((tc, D_MODEL), lambda i: (i, 0)),
        out_shape=jax.ShapeDtypeStruct((n, D_MODEL), F32),
        scratch_shapes=[pltpu.VMEM((2, tc, D_MODEL), F32), pltpu.VMEM((2, tc, D_MODEL), F32),
                        pltpu.SemaphoreType.DMA((2, 2))],
        compiler_params=_params("arbitrary"),
        name="moe_combine",
    )(p1, p2, p1, p2, x2, w1, w2, y, ln_g[None, :], ln_b[None, :])


def _moe_layer(x2, w_router, wg, wu, wd, ln_g, ln_b):
    n = x2.shape[0]
    tm = TM_FFN
    n_tiles = (2 * n) // tm + N_EXPERTS
    p = n_tiles * tm

    meta, counts = _router(x2, w_router)
    e1 = meta[:, 0].astype(jnp.int32)
    e2 = meta[:, 1].astype(jnp.int32)
    w1 = meta[:, 2:3]
    w2 = meta[:, 3:4]
    r1 = meta[:, 4].astype(jnp.int32)
    r2 = meta[:, 5].astype(jnp.int32)
    cnt = counts[0, :N_EXPERTS].astype(jnp.int32)

    tiles_per = (cnt + tm - 1) // tm
    tile_end = jnp.cumsum(tiles_per)
    start = (tile_end - tiles_per) * tm
    pos1 = start[e1] + r1
    pos2 = start[e2] + r2
    num_tiles = tile_end[-1:]
    tile_expert = jnp.minimum(
        jnp.sum((jnp.arange(n_tiles)[:, None] >= tile_end[None, :]).astype(jnp.int32), axis=1),
        N_EXPERTS - 1)
    xs = _dispatch_rows(x2, pos1, pos2, p)
    y = _moe_ffn(xs, tile_expert, num_tiles, wg, wu, wd)
    return _combine(x2, y, pos1, pos2, w1, w2, ln_g, ln_b)


def _prep_in_proj(w):
    sc = w[:, 0:1536]
    cf = w[:, 1536:2560]
    cq = w[:, 2560:2816]
    ckv = w[:, 2816:2944]
    kr = w[:, 2944:2976]
    sb = w[:, 2976:4512]
    gate = w[:, 4512:]
    half = MLA_ROPE // 2
    z_nope = jnp.zeros((D_MODEL, MLA_NOPE), F32)
    z_tail = jnp.zeros((D_MODEL, HEAD_PAD - MLA_NOPE - MLA_ROPE), F32)
    kr_pad = jnp.concatenate([z_nope, kr, z_tail], axis=1)
    kr_rot = jnp.concatenate([z_nope, -kr[:, half:], kr[:, :half], z_tail], axis=1)
    hw = SB_HEADS * SB_HEAD_DIM
    sbq = sb[:, :hw] * (-(SB_HEAD_DIM ** -0.5) * math.log2(math.e))
    mix = jnp.concatenate([sc, cf, sbq, sb[:, hw:], cq, ckv, kr_pad, kr_rot], axis=1)
    return mix.astype(BF16), gate.astype(BF16)


def _prep_mla(w_uq, w_ukv):
    half = MLA_ROPE // 2
    wq = w_uq.reshape(MLA_Q_RANK, MLA_HEADS, MLA_NOPE + MLA_ROPE)
    nope, rope = wq[..., :MLA_NOPE], wq[..., MLA_NOPE:]
    tail = jnp.zeros((MLA_Q_RANK, MLA_HEADS, HEAD_PAD - MLA_NOPE - MLA_ROPE), F32)
    q_main = jnp.concatenate([nope, rope, tail], axis=-1)
    q_rot = jnp.concatenate([jnp.zeros_like(nope), -rope[..., half:], rope[..., :half], tail], axis=-1)
    wkv = w_ukv.reshape(MLA_KV_RANK, MLA_HEADS, MLA_NOPE + MLA_V)
    k_nope, v = wkv[..., :MLA_NOPE], wkv[..., MLA_NOPE:]
    k_main = jnp.concatenate(
        [k_nope, jnp.zeros((MLA_KV_RANK, MLA_HEADS, HEAD_PAD - MLA_NOPE), F32)], axis=-1)
    hw = MLA_HEADS * HEAD_PAD
    return (q_main.reshape(MLA_Q_RANK, hw).astype(BF16), q_rot.reshape(MLA_Q_RANK, hw).astype(BF16),
            k_main.reshape(MLA_KV_RANK, hw).astype(BF16),
            v.reshape(MLA_KV_RANK, MLA_HEADS * MLA_V).astype(BF16))


def kernel(x, positions, w_in, b_gate, sc_conv, cf_conv, cf_conv_bias, cf_ln_g, cf_ln_b,
           mla_q_norm, mla_w_uq, mla_kv_norm, mla_w_ukv, w_branch, w_out,
           ln_mix_g, ln_mix_b, ln_ffn_g, ln_ffn_b, ffn_w_gate, ffn_w_up, ffn_w_down,
           router_w, exp_w_gate, exp_w_up, exp_w_down):
    bsz, seq, d = x.shape
    n = bsz * seq
    cos, sin = _rope_tables(positions)
    x2 = x.reshape(n, d)
    for layer in range(DEPTH):
        w_mix, w_gate = _prep_in_proj(w_in[layer])
        wq, wqr, wk, wv = _prep_mla(mla_w_uq[layer], mla_w_ukv[layer])
        proj, br0, br1 = _inproj_conv(x2, w_mix, seq, sc_conv[layer], cf_conv[layer], cf_conv_bias[layer],
                                      cf_ln_g[layer], cf_ln_b[layer])
        proj = proj.reshape(bsz, seq, N_OUT)
        q, k, v = _mla_prep(proj, cos, sin, mla_q_norm[layer], mla_kv_norm[layer], wq, wqr, wk, wv)
        br2 = _mla_attention(q, k, v)
        br3 = _sb_attention(proj)
        branches = [b.reshape(n, BRANCH_WIDTH) for b in (br0, br1, br2, br3)]
        x2 = _merge(x2, branches, w_gate, b_gate[layer], w_branch[layer].astype(BF16),
                    w_out[layer].astype(BF16), ln_mix_g[layer], ln_mix_b[layer])
        i = layer // 2
        if layer % 2 == 0:
            x2 = _dense_ffn(x2, ffn_w_gate[i].astype(BF16), ffn_w_up[i].astype(BF16),
                            ffn_w_down[i].astype(BF16), ln_ffn_g[layer], ln_ffn_b[layer])
        else:
            x2 = _moe_layer(x2, router_w[i], exp_w_gate[i].astype(BF16), exp_w_up[i].astype(BF16),
                            exp_w_down[i].astype(BF16), ln_ffn_g[layer], ln_ffn_b[layer])
    return x2.reshape(bsz, seq, d)
```

```python
import functools
import math

import jax
import jax.numpy as jnp
from jax import lax
from jax.experimental import pallas as pl
from jax.experimental.pallas import tpu as pltpu

F32 = jnp.float32
BF16 = jnp.bfloat16

D_MODEL = 1024
DEPTH = 2
CHUNK = 64
BRANCH_WIDTH = 512
SC_CONV_LEN = 3
CF_CONV_LEN = 31
MLA_HEADS = 8
MLA_NOPE = 64
MLA_ROPE = 32
MLA_V = 64
MLA_Q_RANK = 256
MLA_KV_RANK = 128
ROPE_THETA = 10000.0
SB_HEADS = 8
SB_HEAD_DIM = 64
D_FF = 3584
N_EXPERTS = 8
DN_ALPHA = (2 * DEPTH) ** 0.25
LN_EPS = 1e-5
RMS_EPS = 1e-6
SB_DEAD_LOG2 = -150.0

LANES = 128
SUBLANES = 8
HEAD_PAD = 128
CONV_HALO = 32
VMEM_LIMIT = 56 * 1024 * 1024

OFF_SCB, OFF_SCC, OFF_SCH = 0, 512, 1024
OFF_CFV, OFF_CFG = 1536, 2048
N_CONV_IN = 2560
OFF_SBQ, OFF_SBK, OFF_SBV = 0, 512, 1024
OFF_CQ = 1536
OFF_CKV = 1792
OFF_KR = 1920
OFF_KRR = 2048
N_OUT = 2176
N_MIX = N_CONV_IN + N_OUT

TM_PROJ = 1024
PROJ_SUB_ROWS = 128
TS_PREP = 512
T_ATT = 512
T_SB = 256
MLA_HEADS_PER_STEP = 4
SB_HEADS_PER_STEP = 8
TM_MERGE = 512
TM_FFN = 512
TF_FFN = 1792
FFN_ROW_CHUNKS = 2
MERGE_ROW_CHUNKS = 2
TM_ROUTE = 512
TT_DISPATCH = 512
TC_COMBINE = 256
DMA_UNROLL = 8

_NT = (((1,), (1,)), ((), ()))


def _params(*sem):
    return pltpu.CompilerParams(dimension_semantics=sem, vmem_limit_bytes=VMEM_LIMIT)


def _sigmoid(x):
    return 1.0 / (1.0 + jnp.exp(-x))


def _layer_norm_rows(v, g, b):
    mu = jnp.mean(v, axis=-1, keepdims=True)
    c = v - mu
    var = jnp.mean(c * c, axis=-1, keepdims=True)
    return c * lax.rsqrt(var + LN_EPS) * g + b


def _rope_kernel(pos_ref, freq_ref, cos_ref, sin_ref):
    ang = pos_ref[0] * freq_ref[...]
    cos_ref[0] = jnp.cos(ang)
    sin_ref[0] = jnp.sin(ang)


def _rope_tables(positions):
    bsz, seq = positions.shape
    half = MLA_ROPE // 2
    inv_freq = 1.0 / (ROPE_THETA ** (jnp.arange(half, dtype=F32) * (2.0 / MLA_ROPE)))
    freq = jnp.concatenate([jnp.zeros((MLA_NOPE,), F32), inv_freq, inv_freq,
                            jnp.zeros((HEAD_PAD - MLA_NOPE - MLA_ROPE,), F32)])[None, :]
    pos = positions.astype(F32)[..., None]
    ts = TS_PREP
    return pl.pallas_call(
        _rope_kernel,
        grid=(bsz, seq // ts),
        in_specs=[pl.BlockSpec((1, ts, 1), lambda b, i: (b, i, 0)),
                  pl.BlockSpec((1, HEAD_PAD), lambda b, i: (0, 0))],
        out_specs=[pl.BlockSpec((1, ts, HEAD_PAD), lambda b, i: (b, i, 0))] * 2,
        out_shape=[jax.ShapeDtypeStruct((bsz, seq, HEAD_PAD), F32)] * 2,
        compiler_params=_params("parallel", "parallel"),
        name="rope_tables",
    )(pos, freq)


def _causal_taps(w_ref, buf, n_taps, base, rows):
    groups = {}
    for l in range(n_taps):
        off = CONV_HALO - (n_taps - 1) + l
        groups.setdefault(off % SUBLANES, []).append((l, off - off % SUBLANES))
    out = None
    for shift, taps in sorted(groups.items()):
        part = None
        for l, lo in taps:
            term = w_ref[l:l + 1, :] * buf[base + lo:base + lo + rows + SUBLANES, :]
            part = term if part is None else part + term
        part = part[shift:shift + rows, :]
        out = part if out is None else out + part
    return out


def _inproj_conv_kernel(x_ref, w_ref, wsc_ref, wcf_ref, bcf_ref, lng_ref, lnb_ref,
                        o_ref, o0_ref, o1_ref, bufc, bufu, *, tm, steps_per_seq):
    i = pl.program_id(0)
    halo = CONV_HALO
    w = BRANCH_WIDTH

    @pl.when(i == 0)
    def _():
        bufc[...] = jnp.zeros(bufc.shape, F32)
        bufu[...] = jnp.zeros(bufu.shape, F32)

    starts_sequence = (i % steps_per_seq) == 0
    bufc[0:halo, :] = jnp.where(starts_sequence, 0.0, bufc[tm:tm + halo, :])
    bufu[0:halo, :] = jnp.where(starts_sequence, 0.0, bufu[tm:tm + halo, :])

    def proj(xb, c0, width):
        return jnp.dot(xb, w_ref[:, c0:c0 + width], preferred_element_type=F32)

    rs = PROJ_SUB_ROWS
    for s0 in range(0, tm, rs):
        rows = slice(s0, s0 + rs)
        xb = x_ref[rows, :].astype(BF16)
        gate_b = proj(xb, OFF_SCB, w)
        bufc[halo + s0:halo + s0 + rs, :] = proj(xb, OFF_SCC, w) * proj(xb, OFF_SCH, w)
        bufu[halo + s0:halo + s0 + rs, :] = proj(xb, OFF_CFV, w) * _sigmoid(proj(xb, OFF_CFG, w))
        for c0 in range(0, N_OUT, 512):
            c1 = min(c0 + 512, N_OUT)
            o_ref[rows, c0:c1] = proj(xb, N_CONV_IN + c0, c1 - c0).astype(BF16)
        o0_ref[rows, :] = (gate_b * _causal_taps(wsc_ref, bufc, SC_CONV_LEN, s0, rs)).astype(BF16)
        y = _layer_norm_rows(_causal_taps(wcf_ref, bufu, CF_CONV_LEN, s0, rs) + bcf_ref[...],
                             lng_ref[...], lnb_ref[...])
        o1_ref[rows, :] = (y * _sigmoid(y)).astype(BF16)


def _inproj_conv(x2, w_mix, seq, sc_conv, cf_conv, cf_bias, cf_ln_g, cf_ln_b):
    n = x2.shape[0]
    tm = TM_PROJ
    w = BRANCH_WIDTH

    def const(shape):
        return pl.BlockSpec(shape, lambda i: (0, 0))

    return pl.pallas_call(
        functools.partial(_inproj_conv_kernel, tm=tm, steps_per_seq=seq // tm),
        grid=(n // tm,),
        in_specs=[pl.BlockSpec((tm, D_MODEL), lambda i: (i, 0)),
                  const((D_MODEL, N_MIX)),
                  const((SC_CONV_LEN, w)), const((CF_CONV_LEN, w)), const((1, w)), const((1, w)), const((1, w))],
        out_specs=[pl.BlockSpec((tm, N_OUT), lambda i: (i, 0)),
                   pl.BlockSpec((tm, w), lambda i: (i, 0)),
                   pl.BlockSpec((tm, w), lambda i: (i, 0))],
        out_shape=[jax.ShapeDtypeStruct((n, N_OUT), BF16),
                   jax.ShapeDtypeStruct((n, w), BF16),
                   jax.ShapeDtypeStruct((n, w), BF16)],
        scratch_shapes=[pltpu.VMEM((CONV_HALO + tm + SUBLANES, w), F32),
                        pltpu.VMEM((CONV_HALO + tm + SUBLANES, w), F32)],
        compiler_params=_params("arbitrary"),
        name="in_proj_conv",
    )(x2, w_mix, sc_conv, cf_conv, cf_bias[None, :], cf_ln_g[None, :], cf_ln_b[None, :])


def _mla_prep_kernel(cq_ref, ckv_ref, kr_ref, krr_ref, cos_ref, sin_ref, qn_ref, kvn_ref,
                     wq_ref, wqr_ref, wk_ref, wv_ref, q_ref, k_ref, v_ref):
    cos = cos_ref[0]
    sin = sin_ref[0]
    cos8 = jnp.concatenate([cos] * MLA_HEADS, axis=1)
    sin8 = jnp.concatenate([sin] * MLA_HEADS, axis=1)
    scale = (MLA_NOPE + MLA_ROPE) ** -0.5 * math.log2(math.e)

    cq = cq_ref[0].astype(F32)
    nq = cq * lax.rsqrt(jnp.mean(cq * cq, axis=-1, keepdims=True) + RMS_EPS) * qn_ref[...]
    nq = nq.astype(BF16)
    q1 = jnp.dot(nq, wq_ref[...], preferred_element_type=F32)
    q2 = jnp.dot(nq, wqr_ref[...], preferred_element_type=F32)
    q_ref[0] = ((q1 * cos8 + q2 * sin8) * scale).astype(BF16)

    ckv = ckv_ref[0].astype(F32)
    nkv = ckv * lax.rsqrt(jnp.mean(ckv * ckv, axis=-1, keepdims=True) + RMS_EPS) * kvn_ref[...]
    nkv = nkv.astype(BF16)
    k_rope = kr_ref[0].astype(F32) * cos + krr_ref[0].astype(F32) * sin
    k_nope = jnp.dot(nkv, wk_ref[...], preferred_element_type=F32)
    k_ref[0] = (k_nope + jnp.concatenate([k_rope] * MLA_HEADS, axis=1)).astype(BF16)
    v_ref[0] = jnp.dot(nkv, wv_ref[...], preferred_element_type=F32).astype(BF16)


def _mla_prep(proj3, cos, sin, q_norm, kv_norm, wq, wqr, wk, wv):
    bsz, seq, _ = proj3.shape
    ts = TS_PREP
    hw = MLA_HEADS * HEAD_PAD
    vw = MLA_HEADS * MLA_V

    def const(shape):
        return pl.BlockSpec(shape, lambda b, i: (0, 0))

    return pl.pallas_call(
        _mla_prep_kernel,
        grid=(bsz, seq // ts),
        in_specs=[pl.BlockSpec((1, ts, MLA_Q_RANK), lambda b, i: (b, i, OFF_CQ // MLA_Q_RANK)),
                  pl.BlockSpec((1, ts, MLA_KV_RANK), lambda b, i: (b, i, OFF_CKV // MLA_KV_RANK)),
                  pl.BlockSpec((1, ts, HEAD_PAD), lambda b, i: (b, i, OFF_KR // HEAD_PAD)),
                  pl.BlockSpec((1, ts, HEAD_PAD), lambda b, i: (b, i, OFF_KRR // HEAD_PAD)),
                  pl.BlockSpec((1, ts, HEAD_PAD), lambda b, i: (b, i, 0)),
                  pl.BlockSpec((1, ts, HEAD_PAD), lambda b, i: (b, i, 0)),
                  const((1, MLA_Q_RANK)), const((1, MLA_KV_RANK)),
                  const((MLA_Q_RANK, hw)), const((MLA_Q_RANK, hw)),
                  const((MLA_KV_RANK, hw)), const((MLA_KV_RANK, vw))],
        out_specs=[pl.BlockSpec((1, ts, hw), lambda b, i: (b, i, 0)),
                   pl.BlockSpec((1, ts, hw), lambda b, i: (b, i, 0)),
                   pl.BlockSpec((1, ts, vw), lambda b, i: (b, i, 0))],
        out_shape=[jax.ShapeDtypeStruct((bsz, seq, hw), BF16),
                   jax.ShapeDtypeStruct((bsz, seq, hw), BF16),
                   jax.ShapeDtypeStruct((bsz, seq, vw), BF16)],
        compiler_params=_params("parallel", "parallel"),
        name="mla_prep",
    )(proj3, proj3, proj3, proj3, cos, sin, q_norm[None, :], kv_norm[None, :], wq, wqr, wk, wv)


def _mla_attn_kernel(q_ref, k_ref, v_ref, o_ref, sa_ref, sb_ref, m_ref, acc_ref, *, t, heads):
    qi = pl.program_id(2)
    row = lax.broadcasted_iota(jnp.int32, (t, t), 0)
    col = lax.broadcasted_iota(jnp.int32, (t, t), 1)
    allowed = (col // CHUNK) <= (row // CHUNK)
    ones = jnp.ones((t, LANES), BF16)
    m_ref[...] = jnp.full(m_ref.shape, -jnp.inf, F32)
    acc_ref[...] = jnp.zeros(acc_ref.shape, F32)

    def scores(kb, s_ref):
        kstart = pl.multiple_of(kb * t, t)
        for hh in range(heads):
            lo = hh * HEAD_PAD
            s_ref[hh] = lax.dot_general(q_ref[0, :, lo:lo + HEAD_PAD],
                                        k_ref[0, pl.ds(kstart, t), lo:lo + HEAD_PAD], _NT,
                                        preferred_element_type=F32)

    def consume(kb, s_ref, masked):
        kstart = pl.multiple_of(kb * t, t)
        for hh in range(heads):
            vlo = (hh // 2) * 2 * MLA_V
            v_ext = jnp.concatenate([v_ref[0, pl.ds(kstart, t), vlo:vlo + 2 * MLA_V], ones], axis=1)
            s = s_ref[hh]
            if masked:
                s = jnp.where(allowed, s, -jnp.inf)
            m_prev = m_ref[hh]
            m_new = jnp.maximum(m_prev, jnp.max(s, axis=1, keepdims=True))
            alpha = jnp.exp2(m_prev - m_new)
            p = jnp.exp2(s - jnp.concatenate([m_new] * (t // LANES), axis=1))
            acc_ref[hh] = (jnp.concatenate([alpha, alpha], axis=1) * acc_ref[hh]
                           + jnp.dot(p.astype(BF16), v_ext, preferred_element_type=F32))
            m_ref[hh] = m_new

    scores(0, sa_ref)

    def body(j, carry):
        scores(2 * j + 1, sb_ref)
        consume(2 * j, sa_ref, False)
        scores(2 * j + 2, sa_ref)
        consume(2 * j + 1, sb_ref, False)
        return carry

    lax.fori_loop(0, lax.shift_right_logical(qi, 1), body, 0)

    @pl.when((qi & 1) == 0)
    def _():
        consume(qi, sa_ref, True)

    @pl.when((qi & 1) == 1)
    def _():
        scores(qi, sb_ref)
        consume(qi - 1, sa_ref, False)
        consume(qi, sb_ref, True)

    pv = 2 * MLA_V
    lane = lax.broadcasted_iota(jnp.int32, (t, pv), 1)
    for pp in range(heads // 2):
        out0 = acc_ref[2 * pp, :, :pv] / acc_ref[2 * pp, :, pv:]
        out1 = acc_ref[2 * pp + 1, :, :pv] / acc_ref[2 * pp + 1, :, pv:]
        o_ref[0, :, pp * pv:(pp + 1) * pv] = jnp.where(lane < MLA_V, out0, out1).astype(BF16)


def _mla_attention(q, k, v):
    bsz, seq, _ = q.shape
    t = T_ATT
    heads = MLA_HEADS_PER_STEP
    pv = 2 * MLA_V
    qw = heads * HEAD_PAD
    vw = heads * MLA_V
    return pl.pallas_call(
        functools.partial(_mla_attn_kernel, t=t, heads=heads),
        grid=(bsz, MLA_HEADS // heads, seq // t),
        in_specs=[pl.BlockSpec((1, t, qw), lambda b, h, i: (b, i, h)),
                  pl.BlockSpec((1, seq, qw), lambda b, h, i: (b, 0, h)),
                  pl.BlockSpec((1, seq, vw), lambda b, h, i: (b, 0, h))],
        out_specs=pl.BlockSpec((1, t, vw), lambda b, h, i: (b, i, h)),
        out_shape=jax.ShapeDtypeStruct((bsz, seq, MLA_HEADS * MLA_V), BF16),
        scratch_shapes=[pltpu.VMEM((heads, t, t), F32), pltpu.VMEM((heads, t, t), F32),
                        pltpu.VMEM((heads, t, LANES), F32),
                        pltpu.VMEM((heads, t, pv + LANES), F32)],
        compiler_params=_params("parallel", "parallel", "arbitrary"),
        name="mla_attention",
    )(q, k, v)


def _sb_attn_kernel(q_ref, k_ref, v_ref, u_ref, o_ref, acc_ref, carry_ref, *, t, heads):
    qi = pl.program_id(2)
    row = lax.broadcasted_iota(jnp.int32, (t, t), 0)
    col = lax.broadcasted_iota(jnp.int32, (t, t), 1)
    before = col < row
    pw = 2 * SB_HEAD_DIM
    lane = lax.broadcasted_iota(jnp.int32, (t, pw), 1)
    reps = t // LANES
    acc_ref[...] = jnp.zeros(acc_ref.shape, F32)
    carry_ref[...] = jnp.zeros(carry_ref.shape, F32)

    def block(kstart, masked):
        for h in range(heads):
            lo = (h // 2) * pw
            q = q_ref[0, :, lo:lo + pw]
            in_head = (lane >= (h % 2) * SB_HEAD_DIM) & (lane < (h % 2 + 1) * SB_HEAD_DIM)
            qm = jnp.where(in_head, q, jnp.zeros_like(q))
            kblk = k_ref[0, pl.ds(kstart, t), lo:lo + pw]
            vblk = v_ref[0, pl.ds(kstart, t), lo:lo + pw]
            nz = lax.dot_general(qm, kblk, _NT, preferred_element_type=F32)
            log_stay = jnp.minimum(nz, 0.0) - jnp.log2(1.0 + jnp.exp2(-jnp.abs(nz)))
            if masked:
                log_stay = jnp.where(before, log_stay, 0.0)
            sums = jnp.dot(log_stay.astype(BF16), u_ref[...], preferred_element_type=F32)
            carry = carry_ref[h]
            own = log_stay - nz
            a = jnp.exp2(sums + (jnp.concatenate([carry] * reps, axis=1) + own))
            if masked:
                a = jnp.where(before, a, 0.0)
            acc_ref[h] += jnp.dot(a.astype(BF16), vblk, preferred_element_type=F32)
            carry_ref[h] = carry + jnp.sum(log_stay, axis=1, keepdims=True)

    block(pl.multiple_of(qi * t, t), True)

    def live():
        return jnp.max(carry_ref[...]) > SB_DEAD_LOG2

    def step(state):
        kb, _ = state
        block(pl.multiple_of(kb * t, t), False)
        return kb - 1, live()

    lax.while_loop(lambda st: (st[0] >= 0) & st[1], step, (qi - 1, live()))

    for pp in range(heads // 2):
        o_ref[0, :, pp * pw:(pp + 1) * pw] = jnp.where(
            lane < SB_HEAD_DIM, acc_ref[2 * pp], acc_ref[2 * pp + 1]).astype(BF16)


def _sb_attention(proj3):
    bsz, seq, _ = proj3.shape
    t = T_SB
    heads = SB_HEADS_PER_STEP
    w = heads * SB_HEAD_DIM
    r = lax.broadcasted_iota(jnp.int32, (t, t), 0)
    c = lax.broadcasted_iota(jnp.int32, (t, t), 1)
    suffix = (r > c).astype(BF16)
    return pl.pallas_call(
        functools.partial(_sb_attn_kernel, t=t, heads=heads),
        grid=(bsz, SB_HEADS // heads, seq // t),
        in_specs=[pl.BlockSpec((1, t, w), lambda b, h, i: (b, i, OFF_SBQ // w + h)),
                  pl.BlockSpec((1, seq, w), lambda b, h, i: (b, 0, OFF_SBK // w + h)),
                  pl.BlockSpec((1, seq, w), lambda b, h, i: (b, 0, OFF_SBV // w + h)),
                  pl.BlockSpec((t, t), lambda b, h, i: (0, 0))],
        out_specs=pl.BlockSpec((1, t, w), lambda b, h, i: (b, i, h)),
        out_shape=jax.ShapeDtypeStruct((bsz, seq, SB_HEADS * SB_HEAD_DIM), BF16),
        scratch_shapes=[pltpu.VMEM((heads, t, 2 * SB_HEAD_DIM), F32), pltpu.VMEM((heads, t, LANES), F32)],
        compiler_params=_params("parallel", "parallel", "arbitrary"),
        name="sb_attention",
    )(proj3, proj3, proj3, suffix)


def _merge_kernel(x_ref, b0_ref, b1_ref, b2_ref, b3_ref, wg_ref, bg_ref, wb_ref, wo_ref,
                  lg_ref, lb_ref, o_ref):
    rows = x_ref.shape[0] // MERGE_ROW_CHUNKS
    for c in range(MERGE_ROW_CHUNKS):
        sl = slice(c * rows, (c + 1) * rows)
        x = x_ref[sl, :]
        xb = x.astype(BF16)
        merged = None
        for n, br in enumerate((b0_ref, b1_ref, b2_ref, b3_ref)):
            c0 = n * D_MODEL
            pre = jnp.dot(xb, wg_ref[:, c0:c0 + D_MODEL], preferred_element_type=F32)
            gate = _sigmoid(pre + bg_ref[:, c0:c0 + D_MODEL])
            term = gate * jnp.dot(br[sl, :], wb_ref[n], preferred_element_type=F32)
            merged = term if merged is None else merged + term
        y = jnp.dot(merged.astype(BF16), wo_ref[...], preferred_element_type=F32)
        o_ref[sl, :] = _layer_norm_rows(DN_ALPHA * x + y, lg_ref[...], lb_ref[...])


def _merge(x2, branches, w_gate, b_gate, w_branch, w_out, ln_g, ln_b):
    n = x2.shape[0]
    tm = TM_MERGE
    nb = len(branches)

    def const2(shape):
        return pl.BlockSpec(shape, lambda i: (0, 0))

    return pl.pallas_call(
        _merge_kernel,
        grid=(n // tm,),
        in_specs=[pl.BlockSpec((tm, D_MODEL), lambda i: (i, 0))]
                 + [pl.BlockSpec((tm, BRANCH_WIDTH), lambda i: (i, 0))] * nb
                 + [const2((D_MODEL, nb * D_MODEL)), const2((1, nb * D_MODEL)),
                    pl.BlockSpec((nb, BRANCH_WIDTH, D_MODEL), lambda i: (0, 0, 0)),
                    const2((D_MODEL, D_MODEL)), const2((1, D_MODEL)), const2((1, D_MODEL))],
        out_specs=pl.BlockSpec((tm, D_MODEL), lambda i: (i, 0)),
        out_shape=jax.ShapeDtypeStruct((n, D_MODEL), F32),
        compiler_params=_params("parallel"),
        name="merge_out",
    )(x2, *branches, w_gate, b_gate[None, :], w_branch, w_out, ln_g[None, :], ln_b[None, :])


def _swiglu_rows(x_rows, wg, wu, wd):
    xb = x_rows.astype(BF16)
    g = jnp.dot(xb, wg, preferred_element_type=F32)
    u = jnp.dot(xb, wu, preferred_element_type=F32)
    h = (g * _sigmoid(g) * u).astype(BF16)
    return jnp.dot(h, wd, preferred_element_type=F32)


def _ffn_accumulate(x_ref, wg, wu, wd, acc_ref, j, finish):
    rows = x_ref.shape[0] // FFN_ROW_CHUNKS
    for c in range(FFN_ROW_CHUNKS):
        sl = slice(c * rows, (c + 1) * rows)
        x_rows = x_ref[sl, :]
        acc = jnp.where(j > 0, acc_ref[sl, :], 0.0) + _swiglu_rows(x_rows, wg, wu, wd)
        acc_ref[sl, :] = acc
        finish(sl, x_rows, acc)


def _dense_ffn_kernel(x_ref, wg_ref, wu_ref, wd_ref, lg_ref, lb_ref, o_ref):
    rows = x_ref.shape[0] // FFN_ROW_CHUNKS
    for c in range(FFN_ROW_CHUNKS):
        sl = slice(c * rows, (c + 1) * rows)
        x_rows = x_ref[sl, :]
        f = _swiglu_rows(x_rows, wg_ref[...], wu_ref[...], wd_ref[...])
        o_ref[sl, :] = _layer_norm_rows(DN_ALPHA * x_rows + f, lg_ref[...], lb_ref[...])


def _dense_ffn(x2, wg, wu, wd, ln_g, ln_b):
    n = x2.shape[0]
    tm = TM_FFN

    def resident(shape):
        return pl.BlockSpec(shape, lambda i: (0, 0), pipeline_mode=pl.Buffered(1))

    return pl.pallas_call(
        _dense_ffn_kernel,
        grid=(n // tm,),
        in_specs=[pl.BlockSpec((tm, D_MODEL), lambda i: (i, 0)),
                  resident((D_MODEL, D_FF)), resident((D_MODEL, D_FF)), resident((D_FF, D_MODEL)),
                  pl.BlockSpec((1, D_MODEL), lambda i: (0, 0)),
                  pl.BlockSpec((1, D_MODEL), lambda i: (0, 0))],
        out_specs=pl.BlockSpec((tm, D_MODEL), lambda i: (i, 0)),
        out_shape=jax.ShapeDtypeStruct((n, D_MODEL), F32),
        compiler_params=_params("parallel"),
        name="dense_ffn",
    )(x2, wg, wu, wd, ln_g[None, :], ln_b[None, :])


def _router_kernel(x_ref, wr_ref, tri_ref, meta_ref, cnt_ref, run_ref):
    @pl.when(pl.program_id(0) == 0)
    def _():
        run_ref[...] = jnp.zeros(run_ref.shape, F32)

    x = x_ref[...]
    w = wr_ref[...]
    x_hi = x.astype(BF16)
    x_lo = (x - x_hi.astype(F32)).astype(BF16)
    w_hi = w.astype(BF16)
    w_lo = (w - w_hi.astype(F32)).astype(BF16)
    logits = (jnp.dot(x_hi, w_hi, preferred_element_type=F32)
              + jnp.dot(x_lo, w_hi, preferred_element_type=F32)
              + jnp.dot(x_hi, w_lo, preferred_element_type=F32))
    lane = lax.broadcasted_iota(jnp.int32, logits.shape, 1)
    logits = jnp.where(lane < N_EXPERTS, logits, -jnp.inf)
    v1 = jnp.max(logits, axis=1, keepdims=True)
    i1 = jnp.min(jnp.where(logits == v1, lane, LANES), axis=1, keepdims=True)
    rest = jnp.where(lane == i1, -jnp.inf, logits)
    v2 = jnp.max(rest, axis=1, keepdims=True)
    i2 = jnp.min(jnp.where(rest == v2, lane, LANES), axis=1, keepdims=True)
    ex = jnp.exp(v2 - v1)
    den = 1.0 + ex
    w1 = 1.0 / den
    w2 = ex / den

    sel1 = lane == i1
    sel2 = lane == i2
    chosen = jnp.where(sel1 | sel2, 1.0, 0.0)
    run = run_ref[...]
    before = jnp.dot(tri_ref[...], chosen.astype(BF16), preferred_element_type=F32) + run
    r1 = jnp.sum(jnp.where(sel1, before, 0.0), axis=1, keepdims=True)
    r2 = jnp.sum(jnp.where(sel2, before, 0.0), axis=1, keepdims=True)
    run = run + jnp.sum(chosen, axis=0, keepdims=True)
    run_ref[...] = run
    cnt_ref[...] = run

    meta = jnp.where(lane == 0, i1.astype(F32), 0.0)
    meta = jnp.where(lane == 1, i2.astype(F32), meta)
    meta = jnp.where(lane == 2, w1, meta)
    meta = jnp.where(lane == 3, w2, meta)
    meta = jnp.where(lane == 4, r1, meta)
    meta = jnp.where(lane == 5, r2, meta)
    meta_ref[...] = meta.T[0:SUBLANES, :]


def _router(x2, w_router):
    n = x2.shape[0]
    tm = TM_ROUTE
    wr = jnp.pad(w_router, ((0, 0), (0, LANES - N_EXPERTS)))
    r = lax.broadcasted_iota(jnp.int32, (tm, tm), 0)
    c = lax.broadcasted_iota(jnp.int32, (tm, tm), 1)
    tri = (c < r).astype(BF16)
    return pl.pallas_call(
        _router_kernel,
        grid=(n // tm,),
        in_specs=[pl.BlockSpec((tm, D_MODEL), lambda i: (i, 0)),
                  pl.BlockSpec((D_MODEL, LANES), lambda i: (0, 0)),
                  pl.BlockSpec((tm, tm), lambda i: (0, 0))],
        out_specs=[pl.BlockSpec((SUBLANES, tm), lambda i: (0, i)),
                   pl.BlockSpec((1, LANES), lambda i: (0, 0))],
        out_shape=[jax.ShapeDtypeStruct((SUBLANES, n), F32),
                   jax.ShapeDtypeStruct((1, LANES), F32)],
        scratch_shapes=[pltpu.VMEM((1, LANES), F32)],
        compiler_params=_params("arbitrary"),
        name="router",
    )(x2, wr, tri)


def _row_copy(src_hbm, dst_ref, src_row, dst_row, sem):
    return pltpu.make_async_copy(src_hbm.at[pl.ds(src_row, 1)], dst_ref.at[pl.ds(dst_row, 1)], sem)


def _all_rows_copy(src_hbm, dst_ref, sem):
    return pltpu.make_async_copy(src_hbm.at[pl.ds(0, dst_ref.shape[0])], dst_ref, sem)


def _dispatch_kernel(p1_ref, p2_ref, x_ref, init_hbm, o_hbm, sem, *, tt):
    del init_hbm

    def chunk(c, carry):
        for u in range(DMA_UNROLL):
            r = c * DMA_UNROLL + u
            src = x_ref.at[pl.ds(r, 1)]
            pltpu.make_async_copy(src, o_hbm.at[pl.ds(p1_ref[0, 0, r], 1)], sem.at[0]).start(priority=0)
            pltpu.make_async_copy(src, o_hbm.at[pl.ds(p2_ref[0, 0, r], 1)], sem.at[1]).start(priority=1)
        return carry

    lax.fori_loop(0, tt // DMA_UNROLL, chunk, 0)
    pltpu.make_async_copy(x_ref, o_hbm.at[pl.ds(0, tt)], sem.at[0]).wait()
    pltpu.make_async_copy(x_ref, o_hbm.at[pl.ds(0, tt)], sem.at[1]).wait()


def _dispatch_rows(x2, pos1, pos2, p):
    n = x2.shape[0]
    tt = TT_DISPATCH
    idx = pl.BlockSpec((1, 1, tt), lambda i: (i, 0, 0), memory_space=pltpu.SMEM)
    return pl.pallas_call(
        functools.partial(_dispatch_kernel, tt=tt),
        grid=(n // tt,),
        in_specs=[idx, idx,
                  pl.BlockSpec((tt, D_MODEL), lambda i: (i, 0)),
                  pl.BlockSpec(memory_space=pl.ANY)],
        out_specs=pl.BlockSpec(memory_space=pl.ANY),
        out_shape=jax.ShapeDtypeStruct((p, D_MODEL), x2.dtype),
        input_output_aliases={3: 0},
        scratch_shapes=[pltpu.SemaphoreType.DMA((2,))],
        compiler_params=_params("arbitrary"),
        name="moe_dispatch",
    )(pos1.reshape(n // tt, 1, tt), pos2.reshape(n // tt, 1, tt), x2, jnp.zeros((p, D_MODEL), x2.dtype))


def _moe_ffn_kernel(te_ref, nt_ref, x_ref, wg_ref, wu_ref, wd_ref, o_ref, acc_ref):
    i = pl.program_id(0)
    j = pl.program_id(1)

    @pl.when((i == 0) & (j == 0))
    def _():
        acc_ref[...] = jnp.zeros(acc_ref.shape, F32)

    @pl.when(i < nt_ref[0])
    def _():
        def finish(sl, x_rows, acc):
            o_ref[sl, :] = acc

        _ffn_accumulate(x_ref, wg_ref[0], wu_ref[0], wd_ref[0], acc_ref, j, finish)

    @pl.when(i >= nt_ref[0])
    def _():
        o_ref[...] = jnp.zeros(o_ref.shape, o_ref.dtype)


def _moe_ffn(xs, tile_expert, num_tiles, wg, wu, wd):
    p = xs.shape[0]
    tm, tf = TM_FFN, TF_FFN
    nj = D_FF // tf

    def row(i, j, te, nt):
        return (jnp.minimum(i, nt[0] - 1), 0)

    def col_j(i, j, nt):
        return jnp.where(i < nt[0], j, nj - 1)

    def w_up(i, j, te, nt):
        return (te[jnp.minimum(i, nt[0] - 1)], 0, col_j(i, j, nt))

    def w_down(i, j, te, nt):
        return (te[jnp.minimum(i, nt[0] - 1)], col_j(i, j, nt), 0)

    grid_spec = pltpu.PrefetchScalarGridSpec(
        num_scalar_prefetch=2,
        grid=(p // tm, nj),
        in_specs=[pl.BlockSpec((tm, D_MODEL), row),
                  pl.BlockSpec((1, D_MODEL, tf), w_up),
                  pl.BlockSpec((1, D_MODEL, tf), w_up),
                  pl.BlockSpec((1, tf, D_MODEL), w_down)],
        out_specs=pl.BlockSpec((tm, D_MODEL), lambda i, j, te, nt: (i, 0)),
        scratch_shapes=[pltpu.VMEM((tm, D_MODEL), F32)],
    )
    return pl.pallas_call(
        _moe_ffn_kernel,
        grid_spec=grid_spec,
        out_shape=jax.ShapeDtypeStruct((p, D_MODEL), F32),
        compiler_params=_params("arbitrary", "arbitrary"),
        name="moe_ffn",
    )(tile_expert, num_tiles, xs, wg, wu, wd)


def _combine_kernel(p1_ref, p2_ref, n1_ref, n2_ref, x_ref, w1_ref, w2_ref, y_hbm, lg_ref, lb_ref, o_ref,
                    buf1, buf2, sem, *, tc):
    i = pl.program_id(0)
    slot = i & 1

    def fetch(a_ref, b_ref, s):
        def chunk(c, carry):
            for u in range(DMA_UNROLL):
                r = c * DMA_UNROLL + u
                _row_copy(y_hbm, buf1.at[s], a_ref[0, 0, r], r, sem.at[0, s]).start(priority=0)
                _row_copy(y_hbm, buf2.at[s], b_ref[0, 0, r], r, sem.at[1, s]).start(priority=1)
            return carry

        lax.fori_loop(0, tc // DMA_UNROLL, chunk, 0)

    @pl.when(i == 0)
    def _():
        fetch(p1_ref, p2_ref, 0)

    @pl.when(i + 1 < pl.num_programs(0))
    def _():
        fetch(n1_ref, n2_ref, 1 - slot)

    _all_rows_copy(y_hbm, buf1.at[slot], sem.at[0, slot]).wait()
    _all_rows_copy(y_hbm, buf2.at[slot], sem.at[1, slot]).wait()
    f = w1_ref[...] * buf1[slot] + w2_ref[...] * buf2[slot]
    o_ref[...] = _layer_norm_rows(DN_ALPHA * x_ref[...] + f, lg_ref[...], lb_ref[...])


def _combine(x2, y, pos1, pos2, w1, w2, ln_g, ln_b):
    n = x2.shape[0]
    tc = TC_COMBINE
    steps = n // tc
    idx = pl.BlockSpec((1, 1, tc), lambda i: (i, 0, 0), memory_space=pltpu.SMEM)
    nxt = pl.BlockSpec((1, 1, tc), lambda i: (jnp.minimum(i + 1, steps - 1), 0, 0), memory_space=pltpu.SMEM)
    p1 = pos1.reshape(steps, 1, tc)
    p2 = pos2.reshape(steps, 1, tc)
    return pl.pallas_call(
        functools.partial(_combine_kernel, tc=tc),
        grid=(steps,),
        in_specs=[idx, idx, nxt, nxt,
                  pl.BlockSpec((tc, D_MODEL), lambda i: (i, 0)),
                  pl.BlockSpec((tc, 1), lambda i: (i, 0)),
                  pl.BlockSpec((tc, 1), lambda i: (i, 0)),
                  pl.BlockSpec(memory_space=pl.ANY),
                  pl.BlockSpec((1, D_MODEL), lambda i: (0, 0)),
                  pl.BlockSpec((1, D_MODEL), lambda i: (0, 0))],
        out_specs=pl.BlockSpec((tc, D_MODEL), lambda i: (i, 0)),
        out_shape=jax.ShapeDtypeStruct((n, D_MODEL), F32),
        scratch_shapes=[pltpu.VMEM((2, tc, D_MODEL), F32), pltpu.VMEM((2, tc, D_MODEL), F32),
                        pltpu.SemaphoreType.DMA((2, 2))],
        compiler_params=_params("arbitrary"),
        name="moe_combine",
    )(p1, p2, p1, p2, x2, w1, w2, y, ln_g[None, :], ln_b[None, :])


def _moe_layer(x2, w_router, wg, wu, wd, ln_g, ln_b):
    n = x2.shape[0]
    tm = TM_FFN
    n_tiles = (2 * n) // tm + N_EXPERTS
    p = n_tiles * tm

    meta, counts = _router(x2, w_router)
    e1 = meta[0].astype(jnp.int32)
    e2 = meta[1].astype(jnp.int32)
    w1 = meta[2][:, None]
    w2 = meta[3][:, None]
    r1 = meta[4].astype(jnp.int32)
    r2 = meta[5].astype(jnp.int32)
    cnt = counts[0, :N_EXPERTS].astype(jnp.int32)

    tiles_per = (cnt + tm - 1) // tm
    tile_end = jnp.cumsum(tiles_per)
    start = (tile_end - tiles_per) * tm
    pos1 = start[e1] + r1
    pos2 = start[e2] + r2
    num_tiles = tile_end[-1:]
    tile_expert = jnp.minimum(
        jnp.sum((jnp.arange(n_tiles)[:, None] >= tile_end[None, :]).astype(jnp.int32), axis=1),
        N_EXPERTS - 1)
    xs = _dispatch_rows(x2, pos1, pos2, p)
    y = _moe_ffn(xs, tile_expert, num_tiles, wg, wu, wd)
    return _combine(x2, y, pos1, pos2, w1, w2, ln_g, ln_b)


def _prep_in_proj(w):
    sc = w[:, 0:1536]
    cf = w[:, 1536:2560]
    cq = w[:, 2560:2816]
    ckv = w[:, 2816:2944]
    kr = w[:, 2944:2976]
    sb = w[:, 2976:4512]
    gate = w[:, 4512:]
    half = MLA_ROPE // 2
    z_nope = jnp.zeros((D_MODEL, MLA_NOPE), F32)
    z_tail = jnp.zeros((D_MODEL, HEAD_PAD - MLA_NOPE - MLA_ROPE), F32)
    kr_pad = jnp.concatenate([z_nope, kr, z_tail], axis=1)
    kr_rot = jnp.concatenate([z_nope, -kr[:, half:], kr[:, :half], z_tail], axis=1)
    hw = SB_HEADS * SB_HEAD_DIM
    sbq = sb[:, :hw] * (-(SB_HEAD_DIM ** -0.5) * math.log2(math.e))
    mix = jnp.concatenate([sc, cf, sbq, sb[:, hw:], cq, ckv, kr_pad, kr_rot], axis=1)
    return mix.astype(BF16), gate.astype(BF16)


def _prep_mla(w_uq, w_ukv):
    half = MLA_ROPE // 2
    wq = w_uq.reshape(MLA_Q_RANK, MLA_HEADS, MLA_NOPE + MLA_ROPE)
    nope, rope = wq[..., :MLA_NOPE], wq[..., MLA_NOPE:]
    tail = jnp.zeros((MLA_Q_RANK, MLA_HEADS, HEAD_PAD - MLA_NOPE - MLA_ROPE), F32)
    q_main = jnp.concatenate([nope, rope, tail], axis=-1)
    q_rot = jnp.concatenate([jnp.zeros_like(nope), -rope[..., half:], rope[..., :half], tail], axis=-1)
    wkv = w_ukv.reshape(MLA_KV_RANK, MLA_HEADS, MLA_NOPE + MLA_V)
    k_nope, v = wkv[..., :MLA_NOPE], wkv[..., MLA_NOPE:]
    k_main = jnp.concatenate(
        [k_nope, jnp.zeros((MLA_KV_RANK, MLA_HEADS, HEAD_PAD - MLA_NOPE), F32)], axis=-1)
    hw = MLA_HEADS * HEAD_PAD
    return (q_main.reshape(MLA_Q_RANK, hw).astype(BF16), q_rot.reshape(MLA_Q_RANK, hw).astype(BF16),
            k_main.reshape(MLA_KV_RANK, hw).astype(BF16),
            v.reshape(MLA_KV_RANK, MLA_HEADS * MLA_V).astype(BF16))


def kernel(x, positions, w_in, b_gate, sc_conv, cf_conv, cf_conv_bias, cf_ln_g, cf_ln_b,
           mla_q_norm, mla_w_uq, mla_kv_norm, mla_w_ukv, w_branch, w_out,
           ln_mix_g, ln_mix_b, ln_ffn_g, ln_ffn_b, ffn_w_gate, ffn_w_up, ffn_w_down,
           router_w, exp_w_gate, exp_w_up, exp_w_down):
    bsz, seq, d = x.shape
    n = bsz * seq
    cos, sin = _rope_tables(positions)
    x2 = x.reshape(n, d)
    for layer in range(DEPTH):
        w_mix, w_gate = _prep_in_proj(w_in[layer])
        wq, wqr, wk, wv = _prep_mla(mla_w_uq[layer], mla_w_ukv[layer])
        proj, br0, br1 = _inproj_conv(x2, w_mix, seq, sc_conv[layer], cf_conv[layer], cf_conv_bias[layer],
                                      cf_ln_g[layer], cf_ln_b[layer])
        proj = proj.reshape(bsz, seq, N_OUT)
        q, k, v = _mla_prep(proj, cos, sin, mla_q_norm[layer], mla_kv_norm[layer], wq, wqr, wk, wv)
        br2 = _mla_attention(q, k, v)
        br3 = _sb_attention(proj)
        branches = [b.reshape(n, BRANCH_WIDTH) for b in (br0, br1, br2, br3)]
        x2 = _merge(x2, branches, w_gate, b_gate[layer], w_branch[layer].astype(BF16),
                    w_out[layer].astype(BF16), ln_mix_g[layer], ln_mix_b[layer])
        i = layer // 2
        if layer % 2 == 0:
            x2 = _dense_ffn(x2, ffn_w_gate[i].astype(BF16), ffn_w_up[i].astype(BF16),
                            ffn_w_down[i].astype(BF16), ln_ffn_g[layer], ln_ffn_b[layer])
        else:
            x2 = _moe_layer(x2, router_w[i], exp_w_gate[i].astype(BF16), exp_w_up[i].astype(BF16),
                            exp_w_down[i].astype(BF16), ln_ffn_g[layer], ln_ffn_b[layer])
    return x2.reshape(bsz, seq, d)
```

```python
import functools
import math

import jax
import jax.numpy as jnp
from jax import lax
from jax.experimental import pallas as pl
from jax.experimental.pallas import tpu as pltpu

F32 = jnp.float32
BF16 = jnp.bfloat16

D_MODEL = 1024
DEPTH = 2
CHUNK = 64
BRANCH_WIDTH = 512
SC_CONV_LEN = 3
CF_CONV_LEN = 31
MLA_HEADS = 8
MLA_NOPE = 64
MLA_ROPE = 32
MLA_V = 64
MLA_Q_RANK = 256
MLA_KV_RANK = 128
ROPE_THETA = 10000.0
SB_HEADS = 8
SB_HEAD_DIM = 64
D_FF = 3584
N_EXPERTS = 8
DN_ALPHA = (2 * DEPTH) ** 0.25
LN_EPS = 1e-5
RMS_EPS = 1e-6
SB_DEAD_LOG2 = -150.0

LANES = 128
SUBLANES = 8
HEAD_PAD = 128
CONV_HALO = 32
VMEM_LIMIT = 56 * 1024 * 1024

OFF_SCB, OFF_SCC, OFF_SCH = 0, 512, 1024
OFF_CFV, OFF_CFG = 1536, 2048
N_CONV_IN = 2560
OFF_SBQ, OFF_SBK, OFF_SBV = 0, 512, 1024
OFF_CQ = 1536
OFF_CKV = 1792
OFF_KR = 1920
OFF_KRR = 2048
N_OUT = 2176
N_MIX = N_CONV_IN + N_OUT

TM_PROJ = 1024
PROJ_SUB_ROWS = 128
TS_PREP = 512
T_ATT = 512
T_SB = 256
MLA_HEADS_PER_STEP = 8
SB_HEADS_PER_STEP = 8
TM_MERGE = 1024
TM_FFN = 512
TF_FFN = 1792
FFN_ROW_CHUNKS = 2
MERGE_ROW_CHUNKS = 4
TM_ROUTE = 512
TT_DISPATCH = 512
TC_COMBINE = 512
DMA_UNROLL = 8

_NT = (((1,), (1,)), ((), ()))


def _params(*sem):
    return pltpu.CompilerParams(dimension_semantics=sem, vmem_limit_bytes=VMEM_LIMIT)


def _sigmoid(x):
    return 1.0 / (1.0 + jnp.exp(-x))


def _layer_norm_rows(v, g, b):
    mu = jnp.mean(v, axis=-1, keepdims=True)
    c = v - mu
    var = jnp.mean(c * c, axis=-1, keepdims=True)
    return c * lax.rsqrt(var + LN_EPS) * g + b


def _rope_kernel(pos_ref, freq_ref, cos_ref, sin_ref):
    ang = pos_ref[0] * freq_ref[...]
    cos_ref[0] = jnp.cos(ang)
    sin_ref[0] = jnp.sin(ang)


def _rope_tables(positions):
    bsz, seq = positions.shape
    half = MLA_ROPE // 2
    inv_freq = 1.0 / (ROPE_THETA ** (jnp.arange(half, dtype=F32) * (2.0 / MLA_ROPE)))
    freq = jnp.concatenate([jnp.zeros((MLA_NOPE,), F32), inv_freq, inv_freq,
                            jnp.zeros((HEAD_PAD - MLA_NOPE - MLA_ROPE,), F32)])[None, :]
    pos = positions.astype(F32)[..., None]
    ts = TS_PREP
    return pl.pallas_call(
        _rope_kernel,
        grid=(bsz, seq // ts),
        in_specs=[pl.BlockSpec((1, ts, 1), lambda b, i: (b, i, 0)),
                  pl.BlockSpec((1, HEAD_PAD), lambda b, i: (0, 0))],
        out_specs=[pl.BlockSpec((1, ts, HEAD_PAD), lambda b, i: (b, i, 0))] * 2,
        out_shape=[jax.ShapeDtypeStruct((bsz, seq, HEAD_PAD), F32)] * 2,
        compiler_params=_params("parallel", "parallel"),
        name="rope_tables",
    )(pos, freq)


def _causal_taps(w_ref, buf, n_taps, base, rows):
    groups = {}
    for l in range(n_taps):
        off = CONV_HALO - (n_taps - 1) + l
        groups.setdefault(off % SUBLANES, []).append((l, off - off % SUBLANES))
    out = None
    for shift, taps in sorted(groups.items()):
        part = None
        for l, lo in taps:
            term = w_ref[l:l + 1, :] * buf[base + lo:base + lo + rows + SUBLANES, :]
            part = term if part is None else part + term
        part = part[shift:shift + rows, :]
        out = part if out is None else out + part
    return out


def _inproj_conv_kernel(x_ref, w_ref, wsc_ref, wcf_ref, bcf_ref, lng_ref, lnb_ref,
                        o_ref, o0_ref, o1_ref, bufc, bufu, *, tm, steps_per_seq):
    i = pl.program_id(0)
    halo = CONV_HALO
    w = BRANCH_WIDTH

    @pl.when(i == 0)
    def _():
        bufc[...] = jnp.zeros(bufc.shape, F32)
        bufu[...] = jnp.zeros(bufu.shape, F32)

    starts_sequence = (i % steps_per_seq) == 0
    bufc[0:halo, :] = jnp.where(starts_sequence, 0.0, bufc[tm:tm + halo, :])
    bufu[0:halo, :] = jnp.where(starts_sequence, 0.0, bufu[tm:tm + halo, :])

    def proj(xb, c0, width):
        return jnp.dot(xb, w_ref[:, c0:c0 + width], preferred_element_type=F32)

    rs = PROJ_SUB_ROWS
    for s0 in range(0, tm, rs):
        rows = slice(s0, s0 + rs)
        xb = x_ref[rows, :].astype(BF16)
        gate_b = proj(xb, OFF_SCB, w)
        bufc[halo + s0:halo + s0 + rs, :] = proj(xb, OFF_SCC, w) * proj(xb, OFF_SCH, w)
        bufu[halo + s0:halo + s0 + rs, :] = proj(xb, OFF_CFV, w) * _sigmoid(proj(xb, OFF_CFG, w))
        for c0 in range(0, N_OUT, 512):
            c1 = min(c0 + 512, N_OUT)
            o_ref[rows, c0:c1] = proj(xb, N_CONV_IN + c0, c1 - c0).astype(BF16)
        o0_ref[rows, :] = (gate_b * _causal_taps(wsc_ref, bufc, SC_CONV_LEN, s0, rs)).astype(BF16)
        y = _layer_norm_rows(_causal_taps(wcf_ref, bufu, CF_CONV_LEN, s0, rs) + bcf_ref[...],
                             lng_ref[...], lnb_ref[...])
        o1_ref[rows, :] = (y * _sigmoid(y)).astype(BF16)


def _inproj_conv(x2, w_mix, seq, sc_conv, cf_conv, cf_bias, cf_ln_g, cf_ln_b):
    n = x2.shape[0]
    tm = TM_PROJ
    w = BRANCH_WIDTH

    def const(shape):
        return pl.BlockSpec(shape, lambda i: (0, 0))

    return pl.pallas_call(
        functools.partial(_inproj_conv_kernel, tm=tm, steps_per_seq=seq // tm),
        grid=(n // tm,),
        in_specs=[pl.BlockSpec((tm, D_MODEL), lambda i: (i, 0)),
                  const((D_MODEL, N_MIX)),
                  const((SC_CONV_LEN, w)), const((CF_CONV_LEN, w)), const((1, w)), const((1, w)), const((1, w))],
        out_specs=[pl.BlockSpec((tm, N_OUT), lambda i: (i, 0)),
                   pl.BlockSpec((tm, w), lambda i: (i, 0)),
                   pl.BlockSpec((tm, w), lambda i: (i, 0))],
        out_shape=[jax.ShapeDtypeStruct((n, N_OUT), BF16),
                   jax.ShapeDtypeStruct((n, w), BF16),
                   jax.ShapeDtypeStruct((n, w), BF16)],
        scratch_shapes=[pltpu.VMEM((CONV_HALO + tm + SUBLANES, w), F32),
                        pltpu.VMEM((CONV_HALO + tm + SUBLANES, w), F32)],
        compiler_params=_params("arbitrary"),
        name="in_proj_conv",
    )(x2, w_mix, sc_conv, cf_conv, cf_bias[None, :], cf_ln_g[None, :], cf_ln_b[None, :])


def _mla_prep_kernel(cq_ref, ckv_ref, kr_ref, krr_ref, cos_ref, sin_ref, qn_ref, kvn_ref,
                     wq_ref, wqr_ref, wk_ref, wv_ref, q_ref, k_ref, v_ref):
    cos = cos_ref[0]
    sin = sin_ref[0]
    cos8 = jnp.concatenate([cos] * MLA_HEADS, axis=1)
    sin8 = jnp.concatenate([sin] * MLA_HEADS, axis=1)
    scale = (MLA_NOPE + MLA_ROPE) ** -0.5 * math.log2(math.e)

    cq = cq_ref[0].astype(F32)
    nq = cq * lax.rsqrt(jnp.mean(cq * cq, axis=-1, keepdims=True) + RMS_EPS) * qn_ref[...]
    nq = nq.astype(BF16)
    q1 = jnp.dot(nq, wq_ref[...], preferred_element_type=F32)
    q2 = jnp.dot(nq, wqr_ref[...], preferred_element_type=F32)
    q_ref[0] = ((q1 * cos8 + q2 * sin8) * scale).astype(BF16)

    ckv = ckv_ref[0].astype(F32)
    nkv = ckv * lax.rsqrt(jnp.mean(ckv * ckv, axis=-1, keepdims=True) + RMS_EPS) * kvn_ref[...]
    nkv = nkv.astype(BF16)
    k_rope = kr_ref[0].astype(F32) * cos + krr_ref[0].astype(F32) * sin
    k_nope = jnp.dot(nkv, wk_ref[...], preferred_element_type=F32)
    k_ref[0] = (k_nope + jnp.concatenate([k_rope] * MLA_HEADS, axis=1)).astype(BF16)
    v_ref[0] = jnp.dot(nkv, wv_ref[...], preferred_element_type=F32).astype(BF16)


def _mla_prep(proj3, cos, sin, q_norm, kv_norm, wq, wqr, wk, wv):
    bsz, seq, _ = proj3.shape
    ts = TS_PREP
    hw = MLA_HEADS * HEAD_PAD
    vw = MLA_HEADS * MLA_V

    def const(shape):
        return pl.BlockSpec(shape, lambda b, i: (0, 0))

    return pl.pallas_call(
        _mla_prep_kernel,
        grid=(bsz, seq // ts),
        in_specs=[pl.BlockSpec((1, ts, MLA_Q_RANK), lambda b, i: (b, i, OFF_CQ // MLA_Q_RANK)),
                  pl.BlockSpec((1, ts, MLA_KV_RANK), lambda b, i: (b, i, OFF_CKV // MLA_KV_RANK)),
                  pl.BlockSpec((1, ts, HEAD_PAD), lambda b, i: (b, i, OFF_KR // HEAD_PAD)),
                  pl.BlockSpec((1, ts, HEAD_PAD), lambda b, i: (b, i, OFF_KRR // HEAD_PAD)),
                  pl.BlockSpec((1, ts, HEAD_PAD), lambda b, i: (b, i, 0)),
                  pl.BlockSpec((1, ts, HEAD_PAD), lambda b, i: (b, i, 0)),
                  const((1, MLA_Q_RANK)), const((1, MLA_KV_RANK)),
                  const((MLA_Q_RANK, hw)), const((MLA_Q_RANK, hw)),
                  const((MLA_KV_RANK, hw)), const((MLA_KV_RANK, vw))],
        out_specs=[pl.BlockSpec((1, ts, hw), lambda b, i: (b, i, 0)),
                   pl.BlockSpec((1, ts, hw), lambda b, i: (b, i, 0)),
                   pl.BlockSpec((1, ts, vw), lambda b, i: (b, i, 0))],
        out_shape=[jax.ShapeDtypeStruct((bsz, seq, hw), BF16),
                   jax.ShapeDtypeStruct((bsz, seq, hw), BF16),
                   jax.ShapeDtypeStruct((bsz, seq, vw), BF16)],
        compiler_params=_params("parallel", "parallel"),
        name="mla_prep",
    )(proj3, proj3, proj3, proj3, cos, sin, q_norm[None, :], kv_norm[None, :], wq, wqr, wk, wv)


def _mla_attn_kernel(q_ref, k_ref, v_ref, o_ref, sa_ref, sb_ref, m_ref, acc_ref, *, t, heads):
    qi = pl.program_id(2)
    row = lax.broadcasted_iota(jnp.int32, (t, t), 0)
    col = lax.broadcasted_iota(jnp.int32, (t, t), 1)
    allowed = (col // CHUNK) <= (row // CHUNK)
    ones = jnp.ones((t, LANES), BF16)
    m_ref[...] = jnp.full(m_ref.shape, -jnp.inf, F32)
    acc_ref[...] = jnp.zeros(acc_ref.shape, F32)

    def scores(kb, s_ref):
        kstart = pl.multiple_of(kb * t, t)
        for hh in range(heads):
            lo = hh * HEAD_PAD
            s_ref[hh] = lax.dot_general(q_ref[0, :, lo:lo + HEAD_PAD],
                                        k_ref[0, pl.ds(kstart, t), lo:lo + HEAD_PAD], _NT,
                                        preferred_element_type=F32)

    def consume(kb, s_ref, masked):
        kstart = pl.multiple_of(kb * t, t)
        for hh in range(heads):
            vlo = (hh // 2) * 2 * MLA_V
            v_ext = jnp.concatenate([v_ref[0, pl.ds(kstart, t), vlo:vlo + 2 * MLA_V], ones], axis=1)
            s = s_ref[hh]
            if masked:
                s = jnp.where(allowed, s, -jnp.inf)
            m_prev = m_ref[hh]
            m_new = jnp.maximum(m_prev, jnp.max(s, axis=1, keepdims=True))
            alpha = jnp.exp2(m_prev - m_new)
            p = jnp.exp2(s - jnp.concatenate([m_new] * (t // LANES), axis=1))
            acc_ref[hh] = (jnp.concatenate([alpha, alpha], axis=1) * acc_ref[hh]
                           + jnp.dot(p.astype(BF16), v_ext, preferred_element_type=F32))
            m_ref[hh] = m_new

    scores(0, sa_ref)

    def body(j, carry):
        scores(2 * j + 1, sb_ref)
        consume(2 * j, sa_ref, False)
        scores(2 * j + 2, sa_ref)
        consume(2 * j + 1, sb_ref, False)
        return carry

    lax.fori_loop(0, lax.shift_right_logical(qi, 1), body, 0)

    @pl.when((qi & 1) == 0)
    def _():
        consume(qi, sa_ref, True)

    @pl.when((qi & 1) == 1)
    def _():
        scores(qi, sb_ref)
        consume(qi - 1, sa_ref, False)
        consume(qi, sb_ref, True)

    pv = 2 * MLA_V
    lane = lax.broadcasted_iota(jnp.int32, (t, pv), 1)
    for pp in range(heads // 2):
        out0 = acc_ref[2 * pp, :, :pv] / acc_ref[2 * pp, :, pv:]
        out1 = acc_ref[2 * pp + 1, :, :pv] / acc_ref[2 * pp + 1, :, pv:]
        o_ref[0, :, pp * pv:(pp + 1) * pv] = jnp.where(lane < MLA_V, out0, out1).astype(BF16)


def _mla_attention(q, k, v):
    bsz, seq, _ = q.shape
    t = T_ATT
    heads = MLA_HEADS_PER_STEP
    pv = 2 * MLA_V
    qw = heads * HEAD_PAD
    vw = heads * MLA_V
    return pl.pallas_call(
        functools.partial(_mla_attn_kernel, t=t, heads=heads),
        grid=(bsz, MLA_HEADS // heads, seq // t),
        in_specs=[pl.BlockSpec((1, t, qw), lambda b, h, i: (b, i, h)),
                  pl.BlockSpec((1, seq, qw), lambda b, h, i: (b, 0, h), pipeline_mode=pl.Buffered(1)),
                  pl.BlockSpec((1, seq, vw), lambda b, h, i: (b, 0, h), pipeline_mode=pl.Buffered(1))],
        out_specs=pl.BlockSpec((1, t, vw), lambda b, h, i: (b, i, h)),
        out_shape=jax.ShapeDtypeStruct((bsz, seq, MLA_HEADS * MLA_V), BF16),
        scratch_shapes=[pltpu.VMEM((heads, t, t), F32), pltpu.VMEM((heads, t, t), F32),
                        pltpu.VMEM((heads, t, LANES), F32),
                        pltpu.VMEM((heads, t, pv + LANES), F32)],
        compiler_params=_params("parallel", "parallel", "arbitrary"),
        name="mla_attention",
    )(q, k, v)


def _sb_attn_kernel(q_ref, k_ref, v_ref, u_ref, o_ref, acc_ref, carry_ref, *, t, heads):
    qi = pl.program_id(2)
    row = lax.broadcasted_iota(jnp.int32, (t, t), 0)
    col = lax.broadcasted_iota(jnp.int32, (t, t), 1)
    before = col < row
    pw = 2 * SB_HEAD_DIM
    lane = lax.broadcasted_iota(jnp.int32, (t, pw), 1)
    reps = t // LANES
    acc_ref[...] = jnp.zeros(acc_ref.shape, F32)
    carry_ref[...] = jnp.zeros(carry_ref.shape, F32)

    def block(kstart, masked):
        for h in range(heads):
            lo = (h // 2) * pw
            q = q_ref[0, :, lo:lo + pw]
            in_head = (lane >= (h % 2) * SB_HEAD_DIM) & (lane < (h % 2 + 1) * SB_HEAD_DIM)
            qm = jnp.where(in_head, q, jnp.zeros_like(q))
            kblk = k_ref[0, pl.ds(kstart, t), lo:lo + pw]
            vblk = v_ref[0, pl.ds(kstart, t), lo:lo + pw]
            nz = lax.dot_general(qm, kblk, _NT, preferred_element_type=F32)
            log_stay = jnp.minimum(nz, 0.0) - jnp.log2(1.0 + jnp.exp2(-jnp.abs(nz)))
            if masked:
                log_stay = jnp.where(before, log_stay, 0.0)
            sums = jnp.dot(log_stay.astype(BF16), u_ref[...], preferred_element_type=F32)
            carry = carry_ref[h]
            own = log_stay - nz
            a = jnp.exp2(sums + (jnp.concatenate([carry] * reps, axis=1) + own))
            if masked:
                a = jnp.where(before, a, 0.0)
            acc_ref[h] += jnp.dot(a.astype(BF16), vblk, preferred_element_type=F32)
            carry_ref[h] = carry + jnp.sum(log_stay, axis=1, keepdims=True)

    block(pl.multiple_of(qi * t, t), True)

    def live():
        return jnp.max(carry_ref[...]) > SB_DEAD_LOG2

    def step(state):
        kb, _ = state
        block(pl.multiple_of(kb * t, t), False)
        return kb - 1, live()

    lax.while_loop(lambda st: (st[0] >= 0) & st[1], step, (qi - 1, live()))

    for pp in range(heads // 2):
        o_ref[0, :, pp * pw:(pp + 1) * pw] = jnp.where(
            lane < SB_HEAD_DIM, acc_ref[2 * pp], acc_ref[2 * pp + 1]).astype(BF16)


def _sb_attention(proj3):
    bsz, seq, _ = proj3.shape
    t = T_SB
    heads = SB_HEADS_PER_STEP
    w = heads * SB_HEAD_DIM
    r = lax.broadcasted_iota(jnp.int32, (t, t), 0)
    c = lax.broadcasted_iota(jnp.int32, (t, t), 1)
    suffix = (r > c).astype(BF16)
    return pl.pallas_call(
        functools.partial(_sb_attn_kernel, t=t, heads=heads),
        grid=(bsz, SB_HEADS // heads, seq // t),
        in_specs=[pl.BlockSpec((1, t, w), lambda b, h, i: (b, i, OFF_SBQ // w + h)),
                  pl.BlockSpec((1, seq, w), lambda b, h, i: (b, 0, OFF_SBK // w + h)),
                  pl.BlockSpec((1, seq, w), lambda b, h, i: (b, 0, OFF_SBV // w + h)),
                  pl.BlockSpec((t, t), lambda b, h, i: (0, 0))],
        out_specs=pl.BlockSpec((1, t, w), lambda b, h, i: (b, i, h)),
        out_shape=jax.ShapeDtypeStruct((bsz, seq, SB_HEADS * SB_HEAD_DIM), BF16),
        scratch_shapes=[pltpu.VMEM((heads, t, 2 * SB_HEAD_DIM), F32), pltpu.VMEM((heads, t, LANES), F32)],
        compiler_params=_params("parallel", "parallel", "arbitrary"),
        name="sb_attention",
    )(proj3, proj3, proj3, suffix)


def _merge_kernel(x_ref, b0_ref, b1_ref, b2_ref, b3_ref, wg_ref, bg_ref, wb_ref, wo_ref,
                  lg_ref, lb_ref, o_ref):
    rows = x_ref.shape[0] // MERGE_ROW_CHUNKS
    for c in range(MERGE_ROW_CHUNKS):
        sl = slice(c * rows, (c + 1) * rows)
        x = x_ref[sl, :]
        xb = x.astype(BF16)
        merged = None
        for n, br in enumerate((b0_ref, b1_ref, b2_ref, b3_ref)):
            c0 = n * D_MODEL
            pre = jnp.dot(xb, wg_ref[:, c0:c0 + D_MODEL], preferred_element_type=F32)
            gate = _sigmoid(pre + bg_ref[:, c0:c0 + D_MODEL])
            term = gate * jnp.dot(br[sl, :], wb_ref[n], preferred_element_type=F32)
            merged = term if merged is None else merged + term
        y = jnp.dot(merged.astype(BF16), wo_ref[...], preferred_element_type=F32)
        o_ref[sl, :] = _layer_norm_rows(DN_ALPHA * x + y, lg_ref[...], lb_ref[...])


def _merge(x2, branches, w_gate, b_gate, w_branch, w_out, ln_g, ln_b):
    n = x2.shape[0]
    tm = TM_MERGE
    nb = len(branches)

    def const2(shape):
        return pl.BlockSpec(shape, lambda i: (0, 0), pipeline_mode=pl.Buffered(1))

    return pl.pallas_call(
        _merge_kernel,
        grid=(n // tm,),
        in_specs=[pl.BlockSpec((tm, D_MODEL), lambda i: (i, 0))]
                 + [pl.BlockSpec((tm, BRANCH_WIDTH), lambda i: (i, 0))] * nb
                 + [const2((D_MODEL, nb * D_MODEL)), const2((1, nb * D_MODEL)),
                    pl.BlockSpec((nb, BRANCH_WIDTH, D_MODEL), lambda i: (0, 0, 0), pipeline_mode=pl.Buffered(1)),
                    const2((D_MODEL, D_MODEL)), const2((1, D_MODEL)), const2((1, D_MODEL))],
        out_specs=pl.BlockSpec((tm, D_MODEL), lambda i: (i, 0)),
        out_shape=jax.ShapeDtypeStruct((n, D_MODEL), F32),
        compiler_params=_params("parallel"),
        name="merge_out",
    )(x2, *branches, w_gate, b_gate[None, :], w_branch, w_out, ln_g[None, :], ln_b[None, :])


def _swiglu_rows(x_rows, wg, wu, wd):
    xb = x_rows.astype(BF16)
    g = jnp.dot(xb, wg, preferred_element_type=F32)
    u = jnp.dot(xb, wu, preferred_element_type=F32)
    h = (g * _sigmoid(g) * u).astype(BF16)
    return jnp.dot(h, wd, preferred_element_type=F32)


def _ffn_accumulate(x_ref, wg, wu, wd, acc_ref, j, finish):
    rows = x_ref.shape[0] // FFN_ROW_CHUNKS
    for c in range(FFN_ROW_CHUNKS):
        sl = slice(c * rows, (c + 1) * rows)
        x_rows = x_ref[sl, :]
        acc = jnp.where(j > 0, acc_ref[sl, :], 0.0) + _swiglu_rows(x_rows, wg, wu, wd)
        acc_ref[sl, :] = acc
        finish(sl, x_rows, acc)


def _dense_ffn_kernel(x_ref, wg_ref, wu_ref, wd_ref, lg_ref, lb_ref, o_ref):
    rows = x_ref.shape[0] // FFN_ROW_CHUNKS
    for c in range(FFN_ROW_CHUNKS):
        sl = slice(c * rows, (c + 1) * rows)
        x_rows = x_ref[sl, :]
        f = _swiglu_rows(x_rows, wg_ref[...], wu_ref[...], wd_ref[...])
        o_ref[sl, :] = _layer_norm_rows(DN_ALPHA * x_rows + f, lg_ref[...], lb_ref[...])


def _dense_ffn(x2, wg, wu, wd, ln_g, ln_b):
    n = x2.shape[0]
    tm = TM_FFN

    def resident(shape):
        return pl.BlockSpec(shape, lambda i: (0, 0), pipeline_mode=pl.Buffered(1))

    return pl.pallas_call(
        _dense_ffn_kernel,
        grid=(n // tm,),
        in_specs=[pl.BlockSpec((tm, D_MODEL), lambda i: (i, 0)),
                  resident((D_MODEL, D_FF)), resident((D_MODEL, D_FF)), resident((D_FF, D_MODEL)),
                  pl.BlockSpec((1, D_MODEL), lambda i: (0, 0)),
                  pl.BlockSpec((1, D_MODEL), lambda i: (0, 0))],
        out_specs=pl.BlockSpec((tm, D_MODEL), lambda i: (i, 0)),
        out_shape=jax.ShapeDtypeStruct((n, D_MODEL), F32),
        compiler_params=_params("parallel"),
        name="dense_ffn",
    )(x2, wg, wu, wd, ln_g[None, :], ln_b[None, :])


def _router_kernel(x_ref, wr_ref, tri_ref, meta_ref, cnt_ref, run_ref):
    @pl.when(pl.program_id(0) == 0)
    def _():
        run_ref[...] = jnp.zeros(run_ref.shape, F32)

    x = x_ref[...]
    w = wr_ref[...]
    x_hi = x.astype(BF16)
    x_lo = (x - x_hi.astype(F32)).astype(BF16)
    w_hi = w.astype(BF16)
    w_lo = (w - w_hi.astype(F32)).astype(BF16)
    logits = (jnp.dot(x_hi, w_hi, preferred_element_type=F32)
              + jnp.dot(x_lo, w_hi, preferred_element_type=F32)
              + jnp.dot(x_hi, w_lo, preferred_element_type=F32))
    lane = lax.broadcasted_iota(jnp.int32, logits.shape, 1)
    logits = jnp.where(lane < N_EXPERTS, logits, -jnp.inf)
    v1 = jnp.max(logits, axis=1, keepdims=True)
    i1 = jnp.min(jnp.where(logits == v1, lane, LANES), axis=1, keepdims=True)
    rest = jnp.where(lane == i1, -jnp.inf, logits)
    v2 = jnp.max(rest, axis=1, keepdims=True)
    i2 = jnp.min(jnp.where(rest == v2, lane, LANES), axis=1, keepdims=True)
    ex = jnp.exp(v2 - v1)
    den = 1.0 + ex
    w1 = 1.0 / den
    w2 = ex / den

    sel1 = lane == i1
    sel2 = lane == i2
    chosen = jnp.where(sel1 | sel2, 1.0, 0.0)
    run = run_ref[...]
    before = jnp.dot(tri_ref[...], chosen.astype(BF16), preferred_element_type=F32) + run
    r1 = jnp.sum(jnp.where(sel1, before, 0.0), axis=1, keepdims=True)
    r2 = jnp.sum(jnp.where(sel2, before, 0.0), axis=1, keepdims=True)
    run = run + jnp.sum(chosen, axis=0, keepdims=True)
    run_ref[...] = run
    cnt_ref[...] = run

    meta = jnp.where(lane == 0, i1.astype(F32), 0.0)
    meta = jnp.where(lane == 1, i2.astype(F32), meta)
    meta = jnp.where(lane == 2, w1, meta)
    meta = jnp.where(lane == 3, w2, meta)
    meta = jnp.where(lane == 4, r1, meta)
    meta = jnp.where(lane == 5, r2, meta)
    meta_ref[...] = meta.T[0:SUBLANES, :]


def _router(x2, w_router):
    n = x2.shape[0]
    tm = TM_ROUTE
    wr = jnp.pad(w_router, ((0, 0), (0, LANES - N_EXPERTS)))
    r = lax.broadcasted_iota(jnp.int32, (tm, tm), 0)
    c = lax.broadcasted_iota(jnp.int32, (tm, tm), 1)
    tri = (c < r).astype(BF16)
    return pl.pallas_call(
        _router_kernel,
        grid=(n // tm,),
        in_specs=[pl.BlockSpec((tm, D_MODEL), lambda i: (i, 0)),
                  pl.BlockSpec((D_MODEL, LANES), lambda i: (0, 0)),
                  pl.BlockSpec((tm, tm), lambda i: (0, 0))],
        out_specs=[pl.BlockSpec((SUBLANES, tm), lambda i: (0, i)),
                   pl.BlockSpec((1, LANES), lambda i: (0, 0))],
        out_shape=[jax.ShapeDtypeStruct((SUBLANES, n), F32),
                   jax.ShapeDtypeStruct((1, LANES), F32)],
        scratch_shapes=[pltpu.VMEM((1, LANES), F32)],
        compiler_params=_params("arbitrary"),
        name="router",
    )(x2, wr, tri)


def _row_copy(src_hbm, dst_ref, src_row, dst_row, sem):
    return pltpu.make_async_copy(src_hbm.at[pl.ds(src_row, 1)], dst_ref.at[pl.ds(dst_row, 1)], sem)


def _all_rows_copy(src_hbm, dst_ref, sem):
    return pltpu.make_async_copy(src_hbm.at[pl.ds(0, dst_ref.shape[0])], dst_ref, sem)


def _dispatch_kernel(p1_ref, p2_ref, x_ref, init_hbm, o_hbm, sem, *, tt):
    del init_hbm

    def chunk(c, carry):
        for u in range(DMA_UNROLL):
            r = c * DMA_UNROLL + u
            src = x_ref.at[pl.ds(r, 1)]
            pltpu.make_async_copy(src, o_hbm.at[pl.ds(p1_ref[0, 0, r], 1)], sem.at[0]).start(priority=0)
            pltpu.make_async_copy(src, o_hbm.at[pl.ds(p2_ref[0, 0, r], 1)], sem.at[1]).start(priority=1)
        return carry

    lax.fori_loop(0, tt // DMA_UNROLL, chunk, 0)
    pltpu.make_async_copy(x_ref, o_hbm.at[pl.ds(0, tt)], sem.at[0]).wait()
    pltpu.make_async_copy(x_ref, o_hbm.at[pl.ds(0, tt)], sem.at[1]).wait()


def _dispatch_rows(x2, pos1, pos2, p):
    n = x2.shape[0]
    tt = TT_DISPATCH
    idx = pl.BlockSpec((1, 1, tt), lambda i: (i, 0, 0), memory_space=pltpu.SMEM)
    return pl.pallas_call(
        functools.partial(_dispatch_kernel, tt=tt),
        grid=(n // tt,),
        in_specs=[idx, idx,
                  pl.BlockSpec((tt, D_MODEL), lambda i: (i, 0)),
                  pl.BlockSpec(memory_space=pl.ANY)],
        out_specs=pl.BlockSpec(memory_space=pl.ANY),
        out_shape=jax.ShapeDtypeStruct((p, D_MODEL), x2.dtype),
        input_output_aliases={3: 0},
        scratch_shapes=[pltpu.SemaphoreType.DMA((2,))],
        compiler_params=_params("arbitrary"),
        name="moe_dispatch",
    )(pos1.reshape(n // tt, 1, tt), pos2.reshape(n // tt, 1, tt), x2, jnp.zeros((p, D_MODEL), x2.dtype))


def _moe_ffn_kernel(te_ref, nt_ref, x_ref, wg_ref, wu_ref, wd_ref, o_ref, acc_ref):
    i = pl.program_id(0)
    j = pl.program_id(1)

    @pl.when((i == 0) & (j == 0))
    def _():
        acc_ref[...] = jnp.zeros(acc_ref.shape, F32)

    @pl.when(i < nt_ref[0])
    def _():
        def finish(sl, x_rows, acc):
            o_ref[sl, :] = acc

        _ffn_accumulate(x_ref, wg_ref[0], wu_ref[0], wd_ref[0], acc_ref, j, finish)

    @pl.when(i >= nt_ref[0])
    def _():
        o_ref[...] = jnp.zeros(o_ref.shape, o_ref.dtype)


def _moe_ffn(xs, tile_expert, num_tiles, wg, wu, wd):
    p = xs.shape[0]
    tm, tf = TM_FFN, TF_FFN
    nj = D_FF // tf

    def row(i, j, te, nt):
        return (jnp.minimum(i, nt[0] - 1), 0)

    def col_j(i, j, nt):
        return jnp.where(i < nt[0], j, nj - 1)

    def w_up(i, j, te, nt):
        return (te[jnp.minimum(i, nt[0] - 1)], 0, col_j(i, j, nt))

    def w_down(i, j, te, nt):
        return (te[jnp.minimum(i, nt[0] - 1)], col_j(i, j, nt), 0)

    grid_spec = pltpu.PrefetchScalarGridSpec(
        num_scalar_prefetch=2,
        grid=(p // tm, nj),
        in_specs=[pl.BlockSpec((tm, D_MODEL), row),
                  pl.BlockSpec((1, D_MODEL, tf), w_up),
                  pl.BlockSpec((1, D_MODEL, tf), w_up),
                  pl.BlockSpec((1, tf, D_MODEL), w_down)],
        out_specs=pl.BlockSpec((tm, D_MODEL), lambda i, j, te, nt: (i, 0)),
        scratch_shapes=[pltpu.VMEM((tm, D_MODEL), F32)],
    )
    return pl.pallas_call(
        _moe_ffn_kernel,
        grid_spec=grid_spec,
        out_shape=jax.ShapeDtypeStruct((p, D_MODEL), F32),
        compiler_params=_params("arbitrary", "arbitrary"),
        name="moe_ffn",
    )(tile_expert, num_tiles, xs, wg, wu, wd)


def _combine_kernel(p1_ref, p2_ref, n1_ref, n2_ref, x_ref, w1_ref, w2_ref, y_hbm, lg_ref, lb_ref, o_ref,
                    buf1, buf2, sem, *, tc):
    i = pl.program_id(0)
    slot = i & 1

    def fetch(a_ref, b_ref, s):
        def chunk(c, carry):
            for u in range(DMA_UNROLL):
                r = c * DMA_UNROLL + u
                _row_copy(y_hbm, buf1.at[s], a_ref[0, 0, r], r, sem.at[0, s]).start(priority=0)
                _row_copy(y_hbm, buf2.at[s], b_ref[0, 0, r], r, sem.at[1, s]).start(priority=1)
            return carry

        lax.fori_loop(0, tc // DMA_UNROLL, chunk, 0)

    @pl.when(i == 0)
    def _():
        fetch(p1_ref, p2_ref, 0)

    @pl.when(i + 1 < pl.num_programs(0))
    def _():
        fetch(n1_ref, n2_ref, 1 - slot)

    _all_rows_copy(y_hbm, buf1.at[slot], sem.at[0, slot]).wait()
    _all_rows_copy(y_hbm, buf2.at[slot], sem.at[1, slot]).wait()
    f = w1_ref[...] * buf1[slot] + w2_ref[...] * buf2[slot]
    o_ref[...] = _layer_norm_rows(DN_ALPHA * x_ref[...] + f, lg_ref[...], lb_ref[...])


def _combine(x2, y, pos1, pos2, w1, w2, ln_g, ln_b):
    n = x2.shape[0]
    tc = TC_COMBINE
    steps = n // tc
    idx = pl.BlockSpec((1, 1, tc), lambda i: (i, 0, 0), memory_space=pltpu.SMEM)
    nxt = pl.BlockSpec((1, 1, tc), lambda i: (jnp.minimum(i + 1, steps - 1), 0, 0), memory_space=pltpu.SMEM)
    p1 = pos1.reshape(steps, 1, tc)
    p2 = pos2.reshape(steps, 1, tc)
    return pl.pallas_call(
        functools.partial(_combine_kernel, tc=tc),
        grid=(steps,),
        in_specs=[idx, idx, nxt, nxt,
                  pl.BlockSpec((tc, D_MODEL), lambda i: (i, 0)),
                  pl.BlockSpec((tc, 1), lambda i: (i, 0)),
                  pl.BlockSpec((tc, 1), lambda i: (i, 0)),
                  pl.BlockSpec(memory_space=pl.ANY),
                  pl.BlockSpec((1, D_MODEL), lambda i: (0, 0)),
                  pl.BlockSpec((1, D_MODEL), lambda i: (0, 0))],
        out_specs=pl.BlockSpec((tc, D_MODEL), lambda i: (i, 0)),
        out_shape=jax.ShapeDtypeStruct((n, D_MODEL), F32),
        scratch_shapes=[pltpu.VMEM((2, tc, D_MODEL), F32), pltpu.VMEM((2, tc, D_MODEL), F32),
                        pltpu.SemaphoreType.DMA((2, 2))],
        compiler_params=_params("arbitrary"),
        name="moe_combine",
    )(p1, p2, p1, p2, x2, w1, w2, y, ln_g[None, :], ln_b[None, :])


def _moe_layer(x2, w_router, wg, wu, wd, ln_g, ln_b):
    n = x2.shape[0]
    tm = TM_FFN
    n_tiles = (2 * n) // tm + N_EXPERTS
    p = n_tiles * tm

    meta, counts = _router(x2, w_router)
    e1 = meta[0].astype(jnp.int32)
    e2 = meta[1].astype(jnp.int32)
    w1 = meta[2][:, None]
    w2 = meta[3][:, None]
    r1 = meta[4].astype(jnp.int32)
    r2 = meta[5].astype(jnp.int32)
    cnt = counts[0, :N_EXPERTS].astype(jnp.int32)

    tiles_per = (cnt + tm - 1) // tm
    tile_end = jnp.cumsum(tiles_per)
    start = (tile_end - tiles_per) * tm
    pos1 = start[e1] + r1
    pos2 = start[e2] + r2
    num_tiles = tile_end[-1:]
    tile_expert = jnp.minimum(
        jnp.sum((jnp.arange(n_tiles)[:, None] >= tile_end[None, :]).astype(jnp.int32), axis=1),
        N_EXPERTS - 1)
    xs = _dispatch_rows(x2, pos1, pos2, p)
    y = _moe_ffn(xs, tile_expert, num_tiles, wg, wu, wd)
    return _combine(x2, y, pos1, pos2, w1, w2, ln_g, ln_b)


def _prep_in_proj(w):
    sc = w[:, 0:1536]
    cf = w[:, 1536:2560]
    cq = w[:, 2560:2816]
    ckv = w[:, 2816:2944]
    kr = w[:, 2944:2976]
    sb = w[:, 2976:4512]
    gate = w[:, 4512:]
    half = MLA_ROPE // 2
    z_nope = jnp.zeros((D_MODEL, MLA_NOPE), F32)
    z_tail = jnp.zeros((D_MODEL, HEAD_PAD - MLA_NOPE - MLA_ROPE), F32)
    kr_pad = jnp.concatenate([z_nope, kr, z_tail], axis=1)
    kr_rot = jnp.concatenate([z_nope, -kr[:, half:], kr[:, :half], z_tail], axis=1)
    hw = SB_HEADS * SB_HEAD_DIM
    sbq = sb[:, :hw] * (-(SB_HEAD_DIM ** -0.5) * math.log2(math.e))
    mix = jnp.concatenate([sc, cf, sbq, sb[:, hw:], cq, ckv, kr_pad, kr_rot], axis=1)
    return mix.astype(BF16), gate.astype(BF16)


def _prep_mla(w_uq, w_ukv):
    half = MLA_ROPE // 2
    wq = w_uq.reshape(MLA_Q_RANK, MLA_HEADS, MLA_NOPE + MLA_ROPE)
    nope, rope = wq[..., :MLA_NOPE], wq[..., MLA_NOPE:]
    tail = jnp.zeros((MLA_Q_RANK, MLA_HEADS, HEAD_PAD - MLA_NOPE - MLA_ROPE), F32)
    q_main = jnp.concatenate([nope, rope, tail], axis=-1)
    q_rot = jnp.concatenate([jnp.zeros_like(nope), -rope[..., half:], rope[..., :half], tail], axis=-1)
    wkv = w_ukv.reshape(MLA_KV_RANK, MLA_HEADS, MLA_NOPE + MLA_V)
    k_nope, v = wkv[..., :MLA_NOPE], wkv[..., MLA_NOPE:]
    k_main = jnp.concatenate(
        [k_nope, jnp.zeros((MLA_KV_RANK, MLA_HEADS, HEAD_PAD - MLA_NOPE), F32)], axis=-1)
    hw = MLA_HEADS * HEAD_PAD
    return (q_main.reshape(MLA_Q_RANK, hw).astype(BF16), q_rot.reshape(MLA_Q_RANK, hw).astype(BF16),
            k_main.reshape(MLA_KV_RANK, hw).astype(BF16),
            v.reshape(MLA_KV_RANK, MLA_HEADS * MLA_V).astype(BF16))


def kernel(x, positions, w_in, b_gate, sc_conv, cf_conv, cf_conv_bias, cf_ln_g, cf_ln_b,
           mla_q_norm, mla_w_uq, mla_kv_norm, mla_w_ukv, w_branch, w_out,
           ln_mix_g, ln_mix_b, ln_ffn_g, ln_ffn_b, ffn_w_gate, ffn_w_up, ffn_w_down,
           router_w, exp_w_gate, exp_w_up, exp_w_down):
    bsz, seq, d = x.shape
    n = bsz * seq
    cos, sin = _rope_tables(positions)
    x2 = x.reshape(n, d)
    for layer in range(DEPTH):
        w_mix, w_gate = _prep_in_proj(w_in[layer])
        wq, wqr, wk, wv = _prep_mla(mla_w_uq[layer], mla_w_ukv[layer])
        proj, br0, br1 = _inproj_conv(x2, w_mix, seq, sc_conv[layer], cf_conv[layer], cf_conv_bias[layer],
                                      cf_ln_g[layer], cf_ln_b[layer])
        proj = proj.reshape(bsz, seq, N_OUT)
        q, k, v = _mla_prep(proj, cos, sin, mla_q_norm[layer], mla_kv_norm[layer], wq, wqr, wk, wv)
        br2 = _mla_attention(q, k, v)
        br3 = _sb_attention(proj)
        branches = [b.reshape(n, BRANCH_WIDTH) for b in (br0, br1, br2, br3)]
        x2 = _merge(x2, branches, w_gate, b_gate[layer], w_branch[layer].astype(BF16),
                    w_out[layer].astype(BF16), ln_mix_g[layer], ln_mix_b[layer])
        i = layer // 2
        if layer % 2 == 0:
            x2 = _dense_ffn(x2, ffn_w_gate[i].astype(BF16), ffn_w_up[i].astype(BF16),
                            ffn_w_down[i].astype(BF16), ln_ffn_g[layer], ln_ffn_b[layer])
        else:
            x2 = _moe_layer(x2, router_w[i], exp_w_gate[i].astype(BF16), exp_w_up[i].astype(BF16),
                            exp_w_down[i].astype(BF16), ln_ffn_g[layer], ln_ffn_b[layer])
    return x2.reshape(bsz, seq, d)
```
